```python
import jax
import jax.numpy as jnp
from jax import lax
import numpy as np

D_MODEL = 1024
BATCH = 32
SEQ = 2048
DEPTH = 1

GRID_W = 64
CTX_LEN = 256
HEAD_DIM = 64
RWKV_HEADS = 8
RWKV_DIM = RWKV_HEADS * HEAD_DIM
ATT_Q_HEADS = 8
ATT_KV_HEADS = 2
ATT_GROUP = ATT_Q_HEADS // ATT_KV_HEADS
ATT_DIM = ATT_Q_HEADS * HEAD_DIM
ATT_KV_DIM = ATT_KV_HEADS * HEAD_DIM
WINDOW = 128
BLOCK = 128
N_DIR = 2
DECAY_LORA = 64
AAA_LORA = 64
GATE_LORA = 128
N_BRANCH = 2
N_EXPERTS = 16
EC_FACTOR = 2
D_FF_EXPERT = 1536
CONV_W = 3
ROPE_BASE = 10000.0
NORM_EPS = 1e-6
LNX_EPS = 64e-5
MASK_VALUE = -1e30
SPLIT_REST = (N_DIR * DECAY_LORA, N_DIR * AAA_LORA, GATE_LORA, ATT_DIM, ATT_KV_DIM, ATT_KV_DIM, N_BRANCH * D_MODEL)
D_IN = 3 * RWKV_DIM + N_DIR * DECAY_LORA + N_DIR * AAA_LORA + GATE_LORA + ATT_DIM + 2 * ATT_KV_DIM + N_BRANCH * D_MODEL

kernel_name = "hybrid_rwkv7_swa_ec_moe_dit_layer"


def rmsnorm(x, g):
    xf = x.astype(jnp.float32)
    y = xf * lax.rsqrt(jnp.mean(xf * xf, axis=-1, keepdims=True) + NORM_EPS)
    return (y * g.astype(jnp.float32)).astype(x.dtype)


def modulate(h, shift, scale):
    return h * (1 + scale) + shift


def centred_short_conv(x, w):
    xp = jnp.pad(x, ((0, 0), (1, 1), (0, 0)))
    return w[0] * xp[:, :-2] + w[1] * xp[:, 1:-1] + w[2] * xp[:, 2:]


def project(h, w_in, conv_w):
    z = h @ w_in
    rkv = centred_short_conv(z[..., :3 * RWKV_DIM], conv_w)
    r, k, v = jnp.split(rkv, 3, axis=-1)
    rest = z[..., 3 * RWKV_DIM:]
    points = []
    acc = 0
    for size in SPLIT_REST[:-1]:
        acc += size
        points.append(acc)
    wd, ad, gd, q, kq, vq, gates = jnp.split(rest, points, axis=-1)
    return (r, k, v, wd, ad, gd, q, kq, vq, gates)


def rope_1d(x, pos):
    half = x.shape[-1] // 2
    inv_freq = ROPE_BASE ** (-jnp.arange(half, dtype=jnp.float32) / half)
    ang = pos.astype(jnp.float32)[:, None] * inv_freq[None, :]
    cos = jnp.cos(ang)[None, :, None, :]
    sin = jnp.sin(ang)[None, :, None, :]
    xf = x.astype(jnp.float32)
    x1, x2 = xf[..., :half], xf[..., half:]
    return jnp.concatenate([x1 * cos - x2 * sin, x1 * sin + x2 * cos], axis=-1).astype(x.dtype)


def rope_2d(x, row_pos, col_pos):
    d = x.shape[-1] // 2
    return jnp.concatenate([rope_1d(x[..., :d], row_pos), rope_1d(x[..., d:], col_pos)], axis=-1)


def to_heads(t):
    return t.reshape(t.shape[0], t.shape[1], RWKV_HEADS, HEAD_DIM)


def rwkv_direction_inputs(parts, d, p):
    r, k, v, wd, ad = (t.astype(jnp.float32) for t in parts[:5])
    wd_d = wd[..., d * DECAY_LORA:(d + 1) * DECAY_LORA]
    ad_d = ad[..., d * AAA_LORA:(d + 1) * AAA_LORA]
    w_log = -jax.nn.softplus(-(p["rwkv_w0"][d] + jnp.tanh(wd_d) @ p["rwkv_w2"][d])) - 0.5
    decay = jnp.exp(-jnp.exp(w_log))
    a = jax.nn.sigmoid(p["rwkv_a0"][d] + ad_d @ p["rwkv_a2"][d])
    kk = to_heads(k * p["rwkv_kk"][d])
    kk = kk * lax.rsqrt(jnp.maximum(jnp.sum(kk * kk, axis=-1, keepdims=True), 1e-24))
    k_mod = k * (1.0 + (a - 1.0) * p["rwkv_ka"][d])
    return (to_heads(r), to_heads(decay), to_heads(k_mod), to_heads(v), -kk, kk * to_heads(a))


def wkv7_scan(inputs, state0, reverse):
    xs = tuple(jnp.moveaxis(t, 1, 0) for t in inputs)

    def step(S, inp):
        r_t, w_t, k_t, v_t, a_t, b_t = inp
        Sa = jnp.einsum("bhvk,bhk->bhv", S, a_t)
        S = S * w_t[:, :, None, :] + Sa[..., :, None] * b_t[..., None, :] + v_t[..., :, None] * k_t[..., None, :]
        return S, jnp.einsum("bhvk,bhk->bhv", S, r_t)

    S_final, ys = lax.scan(step, state0, xs, reverse=reverse)
    return jnp.moveaxis(ys, 0, 1), S_final


def rwkv_bonus(inputs, r_k):
    r, _, k_mod, v = inputs[:4]
    return jnp.sum(r * k_mod * r_k.astype(jnp.float32), axis=-1, keepdims=True) * v


def rwkv_output(ys, bonus, gd, p):
    B, L = ys.shape[:2]
    mean = jnp.mean(ys, axis=-1, keepdims=True)
    var = jnp.mean(jnp.square(ys - mean), axis=-1, keepdims=True)
    y = ((ys - mean) * lax.rsqrt(var + LNX_EPS)).reshape(B, L, RWKV_DIM)
    y = y * p["lnx_w"] + p["lnx_b"] + bonus.reshape(B, L, RWKV_DIM)
    g = jax.nn.sigmoid(gd.astype(jnp.float32)) @ p["rwkv_g2"]
    return y * g


def rwkv_branch(lat, ctx, p, update_ctx):
    B = lat[0].shape[0]
    state0 = jnp.zeros((B, RWKV_HEADS, HEAD_DIM, HEAD_DIM), jnp.float32)
    y_lat, b_lat, y_ctx, b_ctx = [], [], [], []
    for d in range(N_DIR):
        reverse = d == 1
        inp_ctx = rwkv_direction_inputs(ctx, d, p)
        inp_lat = rwkv_direction_inputs(lat, d, p)
        ys_c, state_ctx = wkv7_scan(inp_ctx, state0, reverse)
        ys_l, _ = wkv7_scan(inp_lat, state_ctx, reverse)
        y_lat.append(ys_l)
        b_lat.append(rwkv_bonus(inp_lat, p["rwkv_rk"][d]))
        if update_ctx:
            y_ctx.append(ys_c)
            b_ctx.append(rwkv_bonus(inp_ctx, p["rwkv_rk"][d]))
    out_lat = rwkv_output(y_lat[0] + y_lat[1], b_lat[0] + b_lat[1], lat[5], p).astype(lat[0].dtype)
    out_ctx = None
    if update_ctx:
        out_ctx = rwkv_output(y_ctx[0] + y_ctx[1], b_ctx[0] + b_ctx[1], ctx[5], p).astype(ctx[0].dtype)
    return out_lat, out_ctx


def attention_branch(lat, ctx, p, update_ctx):
    q, k, v = lat
    q_c, k_c, v_c = ctx
    B, L, _ = q.shape
    Lc = k_c.shape[1]
    rows = L // GRID_W
    row_pos = jnp.repeat(jnp.arange(rows, dtype=jnp.int32), GRID_W)
    col_pos = jnp.tile(jnp.arange(GRID_W, dtype=jnp.int32), rows)
    q = rope_2d(rmsnorm(q.reshape(B, L, ATT_Q_HEADS, HEAD_DIM), p["q_norm_g"]), row_pos, col_pos)
    k = rope_2d(rmsnorm(k.reshape(B, L, ATT_KV_HEADS, HEAD_DIM), p["k_norm_g"]), row_pos, col_pos)
    v = v.reshape(B, L, ATT_KV_HEADS, HEAD_DIM)
    kc = rmsnorm(k_c.reshape(B, Lc, ATT_KV_HEADS, HEAD_DIM), p["k_norm_g"])
    vc = v_c.reshape(B, Lc, ATT_KV_HEADS, HEAD_DIM)
    sink = p["attn_sink"].astype(jnp.float32).reshape(1, ATT_KV_HEADS, ATT_GROUP, 1, 1)
    scale = HEAD_DIM ** -0.5

    n_blocks = L // BLOCK
    pad = ((0, 0), (BLOCK, BLOCK), (0, 0), (0, 0))
    kp = jnp.pad(k, pad)
    vp = jnp.pad(v, pad)
    q_blocks = jnp.moveaxis(q.reshape(B, n_blocks, BLOCK, ATT_KV_HEADS, ATT_GROUP, HEAD_DIM), 1, 0)
    band = 3 * BLOCK

    def block_attend(args):
        i, q_i = args
        start = i * BLOCK
        k_i = lax.dynamic_slice_in_dim(kp, start, band, axis=1)
        v_i = lax.dynamic_slice_in_dim(vp, start, band, axis=1)
        q_pos = start + jnp.arange(BLOCK)
        k_pos = start - BLOCK + jnp.arange(band)
        allowed = ((jnp.abs(q_pos[:, None] - k_pos[None, :]) <= WINDOW)
                   & (k_pos[None, :] >= 0) & (k_pos[None, :] < L))
        s_win = jnp.einsum("bqhgd,bkhd->bhgqk", q_i, k_i).astype(jnp.float32) * scale
        s_win = jnp.where(allowed, s_win, MASK_VALUE)
        s_ctx = jnp.einsum("bqhgd,bkhd->bhgqk", q_i, kc).astype(jnp.float32) * scale
        s_sink = jnp.broadcast_to(sink, s_win.shape[:-1] + (1,))
        probs = jax.nn.softmax(jnp.concatenate([s_win, s_ctx, s_sink], axis=-1), axis=-1).astype(v.dtype)
        o = jnp.einsum("bhgqk,bkhd->bqhgd", probs[..., :band], v_i)
        return o + jnp.einsum("bhgqk,bkhd->bqhgd", probs[..., band:band + Lc], vc)

    out_lat = lax.map(block_attend, (jnp.arange(n_blocks, dtype=jnp.int32), q_blocks))
    out_lat = jnp.moveaxis(out_lat, 0, 1).reshape(B, L, ATT_DIM)

    out_ctx = None
    if update_ctx:
        qc = rmsnorm(q_c.reshape(B, Lc, ATT_Q_HEADS, HEAD_DIM), p["q_norm_g"])
        qc = qc.reshape(B, Lc, ATT_KV_HEADS, ATT_GROUP, HEAD_DIM)
        s = jnp.einsum("bqhgd,bkhd->bhgqk", qc, kc).astype(jnp.float32) * scale
        s_sink = jnp.broadcast_to(sink, s.shape[:-1] + (1,))
        probs = jax.nn.softmax(jnp.concatenate([s, s_sink], axis=-1), axis=-1)[..., :Lc].astype(vc.dtype)
        out_ctx = jnp.einsum("bhgqk,bkhd->bqhgd", probs, vc).reshape(B, Lc, ATT_DIM)
    return out_lat, out_ctx


def merge_branches(o_rwkv, o_attn, gate_logits, p):
    gate_rwkv, gate_attn = jnp.split(jax.nn.sigmoid(gate_logits), N_BRANCH, axis=-1)
    merged = gate_rwkv * (o_rwkv @ p["w_br_rwkv"]) + gate_attn * (o_attn @ p["w_br_attn"])
    return merged @ p["w_out"]


def ec_moe(h, p):
    B, L, D = h.shape
    cap = EC_FACTOR * L // N_EXPERTS
    aff = jax.nn.softmax((h @ p["router_w"]).astype(jnp.float32), axis=-1)
    vals, idx = lax.top_k(jnp.swapaxes(aff, 1, 2), cap)
    idx_e = jnp.swapaxes(idx, 0, 1)
    vals_e = jnp.swapaxes(vals, 0, 1)

    def expert(args):
        w_gate, w_up, w_down, idx_b, val_b = args
        xe = jax.vmap(lambda hb, ib: hb[ib])(h, idx_b)
        ye = (jax.nn.silu(xe @ w_gate) * (xe @ w_up)) @ w_down
        return (ye * val_b[..., None]).astype(h.dtype)

    ys = lax.map(expert, (p["exp_w_gate"], p["exp_w_up"], p["exp_w_down"], idx_e, vals_e))
    flat_idx = (jnp.arange(B, dtype=jnp.int32)[None, :, None] * L + idx_e).reshape(-1)
    out = jnp.zeros((B * L, D), h.dtype).at[flat_idx].add(ys.reshape(-1, D))
    return out.reshape(B, L, D)


def trunk_layer(x, ctx, c_silu, c_ctx_silu, p, update_ctx):
    mod_lat = (c_silu @ p["ada_w"] + p["ada_b"])[:, None, :]
    mod_ctx = (c_ctx_silu @ p["ada_w"] + p["ada_b"])[None, None, :]
    sh1, sc1, gt1, sh2, sc2, gt2 = jnp.split(mod_lat, 6, axis=-1)
    csh1, csc1, cgt1, csh2, csc2, cgt2 = jnp.split(mod_ctx, 6, axis=-1)

    pl = project(modulate(rmsnorm(x, p["norm1_g"]), sh1, sc1), p["w_in"], p["rwkv_conv"])
    pc = project(modulate(rmsnorm(ctx, p["norm1_g"]), csh1, csc1), p["w_in"], p["rwkv_conv"])
    rw_lat, rw_ctx = rwkv_branch(pl[:6], pc[:6], p, update_ctx)
    at_lat, at_ctx = attention_branch(pl[6:9], pc[6:9], p, update_ctx)
    x = x + gt1 * merge_branches(rw_lat, at_lat, pl[9], p)
    x = x + gt2 * ec_moe(modulate(rmsnorm(x, p["norm2_g"]), sh2, sc2), p)
    if update_ctx:
        ctx = ctx + cgt1 * merge_branches(rw_ctx, at_ctx, pc[9], p)
        ctx = ctx + cgt2 * ec_moe(modulate(rmsnorm(ctx, p["norm2_g"]), csh2, csc2), p)
    return x, ctx


def setup_inputs(seed: int = 0) -> dict:
    key = jax.random.key(seed)
    ks = jax.random.split(key, 32)

    def nrm(k, shape, scale):
        return jax.random.normal(k, shape, jnp.float32) * scale

    centre = jnp.array([0.0, 1.0, 0.0], jnp.float32)[None, :, None]
    return {
        "x": nrm(ks[0], (BATCH, SEQ, D_MODEL), 1.0),
        "c": nrm(ks[1], (BATCH, D_MODEL), 1.0),
        "ctx": nrm(ks[2], (BATCH, CTX_LEN, D_MODEL), 1.0),
        "c_ctx": nrm(ks[3], (D_MODEL,), 1.0),
        "ada_w": nrm(ks[4], (DEPTH, D_MODEL, 6 * D_MODEL), 0.5 * D_MODEL ** -0.5),
        "ada_b": nrm(ks[5], (DEPTH, 6 * D_MODEL), 0.02),
        "norm1_g": 1.0 + nrm(ks[6], (DEPTH, D_MODEL), 0.02),
        "norm2_g": 1.0 + nrm(ks[7], (DEPTH, D_MODEL), 0.02),
        "w_in": nrm(ks[8], (DEPTH, D_MODEL, D_IN), D_MODEL ** -0.5),
        "rwkv_conv": centre + nrm(ks[9], (DEPTH, CONV_W, 3 * RWKV_DIM), 0.2),
        "rwkv_w0": jax.random.uniform(ks[10], (DEPTH, N_DIR, RWKV_DIM), jnp.float32, -3.0, 1.0),
        "rwkv_w2": nrm(ks[11], (DEPTH, N_DIR, DECAY_LORA, RWKV_DIM), 0.1),
        "rwkv_a0": nrm(ks[12], (DEPTH, N_DIR, RWKV_DIM), 0.5),
        "rwkv_a2": nrm(ks[13], (DEPTH, N_DIR, AAA_LORA, RWKV_DIM), AAA_LORA ** -0.5),
        "rwkv_kk": 0.85 + nrm(ks[14], (DEPTH, N_DIR, RWKV_DIM), 0.05),
        "rwkv_ka": 1.0 + nrm(ks[15], (DEPTH, N_DIR, RWKV_DIM), 0.05),
        "rwkv_rk": nrm(ks[16], (DEPTH, N_DIR, RWKV_HEADS, HEAD_DIM), 0.1),
        "rwkv_g2": nrm(ks[17], (DEPTH, GATE_LORA, RWKV_DIM), GATE_LORA ** -0.5),
        "lnx_w": 1.0 + nrm(ks[18], (DEPTH, RWKV_DIM), 0.02),
        "lnx_b": nrm(ks[19], (DEPTH, RWKV_DIM), 0.02),
        "q_norm_g": 1.0 + nrm(ks[20], (DEPTH, HEAD_DIM), 0.02),
        "k_norm_g": 1.0 + nrm(ks[21], (DEPTH, HEAD_DIM), 0.02),
        "attn_sink": nrm(ks[22], (DEPTH, ATT_Q_HEADS), 1.0),
        "w_br_rwkv": nrm(ks[23], (DEPTH, RWKV_DIM, D_MODEL), RWKV_DIM ** -0.5),
        "w_br_attn": nrm(ks[24], (DEPTH, ATT_DIM, D_MODEL), ATT_DIM ** -0.5),
        "w_out": nrm(ks[25], (DEPTH, D_MODEL, D_MODEL), D_MODEL ** -0.5),
        "router_w": nrm(ks[26], (DEPTH, D_MODEL, N_EXPERTS), D_MODEL ** -0.5),
        "exp_w_gate": nrm(ks[27], (DEPTH, N_EXPERTS, D_MODEL, D_FF_EXPERT), D_MODEL ** -0.5),
        "exp_w_up": nrm(ks[28], (DEPTH, N_EXPERTS, D_MODEL, D_FF_EXPERT), D_MODEL ** -0.5),
        "exp_w_down": nrm(ks[29], (DEPTH, N_EXPERTS, D_FF_EXPERT, D_MODEL), D_FF_EXPERT ** -0.5),
    }


def reference(x, c, ctx, c_ctx, ada_w, ada_b, norm1_g, norm2_g, w_in, rwkv_conv, rwkv_w0, rwkv_w2,
              rwkv_a0, rwkv_a2, rwkv_kk, rwkv_ka, rwkv_rk, rwkv_g2, lnx_w, lnx_b, q_norm_g, k_norm_g,
              attn_sink, w_br_rwkv, w_br_attn, w_out, router_w, exp_w_gate, exp_w_up, exp_w_down):
    c_silu = jax.nn.silu(c)
    c_ctx_silu = jax.nn.silu(c_ctx)
    for layer in range(DEPTH):
        p = {
            "ada_w": ada_w[layer], "ada_b": ada_b[layer],
            "norm1_g": norm1_g[layer], "norm2_g": norm2_g[layer],
            "w_in": w_in[layer], "rwkv_conv": rwkv_conv[layer],
            "rwkv_w0": rwkv_w0[layer], "rwkv_w2": rwkv_w2[layer],
            "rwkv_a0": rwkv_a0[layer], "rwkv_a2": rwkv_a2[layer],
            "rwkv_kk": rwkv_kk[layer], "rwkv_ka": rwkv_ka[layer], "rwkv_rk": rwkv_rk[layer],
            "rwkv_g2": rwkv_g2[layer], "lnx_w": lnx_w[layer], "lnx_b": lnx_b[layer],
            "q_norm_g": q_norm_g[layer], "k_norm_g": k_norm_g[layer], "attn_sink": attn_sink[layer],
            "w_br_rwkv": w_br_rwkv[layer], "w_br_attn": w_br_attn[layer], "w_out": w_out[layer],
            "router_w": router_w[layer], "exp_w_gate": exp_w_gate[layer],
            "exp_w_up": exp_w_up[layer], "exp_w_down": exp_w_down[layer],
        }
        x, ctx = trunk_layer(x, ctx, c_silu, c_ctx_silu, p, layer < DEPTH - 1)
    return x
```

```python
import functools

import jax
import jax.numpy as jnp
from jax import lax
from jax.experimental import pallas as pl
from jax.experimental.pallas import tpu as pltpu

F32 = jnp.float32
BF16 = jnp.bfloat16
HIGHEST = lax.Precision.HIGHEST

HEAD_DIM = 64
HEAD_SHIFT = 6
GRID_W = 64
WINDOW = 128
ATT_BLOCK = 128
ROPE_BASE = 10000.0
NORM_EPS = 1e-6
LNX_EPS = 64e-5
MASK_VALUE = -1e30
EC_FACTOR = 2

LANES = 128
MXU_DIM = 256
VMEM_LIMIT_BYTES = 56 * 1024 * 1024

WKV_CHUNK = 64
HEADS_PER_STACK = MXU_DIM // HEAD_DIM


def _dot(a, b):
    return jnp.dot(a, b, preferred_element_type=F32)


def _dot_nt(a, b):
    return lax.dot_general(a, b, (((1,), (1,)), ((), ())), preferred_element_type=F32)


def _dot_tn(a, b):
    return lax.dot_general(a, b, (((0,), (0,)), ((), ())), preferred_element_type=F32)


def _sigmoid(x):
    return 1.0 / (1.0 + jnp.exp(-x))


def _split_bf16(x, parts):
    out = []
    rem = x
    for _ in range(parts):
        hi = rem.astype(BF16)
        out.append(hi)
        rem = rem - hi.astype(F32)
    return out


def _dot_split_rhs(a_bf16, x, parts):
    acc = None
    for term in _split_bf16(x, parts):
        d = _dot(a_bf16, term)
        acc = d if acc is None else acc + d
    return acc


def _dot_split_lhs(x, b_bf16, parts):
    acc = None
    for term in _split_bf16(x, parts):
        d = _dot(term, b_bf16)
        acc = d if acc is None else acc + d
    return acc


def _head_ones(width):
    r = lax.broadcasted_iota(jnp.int32, (width, width), 0) >> HEAD_SHIFT
    c = lax.broadcasted_iota(jnp.int32, (width, width), 1) >> HEAD_SHIFT
    return jnp.where(r == c, 1.0, 0.0).astype(BF16)


def _head_sum(x, ones_bd):
    return _dot_split_lhs(x, ones_bd, 2)


def _params(semantics):
    return pltpu.CompilerParams(dimension_semantics=semantics, vmem_limit_bytes=VMEM_LIMIT_BYTES)


def _adaln_kernel(c_ref, w_ref, b_ref, o_ref):
    c = c_ref[...]
    s = c * _sigmoid(c)
    o_ref[...] = jnp.dot(s, w_ref[...], precision=HIGHEST, preferred_element_type=F32) + b_ref[...]


def _adaln(c_all, ada_w, ada_b):
    rows, d = c_all.shape
    n = ada_w.shape[1]
    tn = 512
    return pl.pallas_call(
        _adaln_kernel,
        grid=(n // tn,),
        in_specs=[
            pl.BlockSpec((rows, d), lambda j: (0, 0)),
            pl.BlockSpec((d, tn), lambda j: (0, j)),
            pl.BlockSpec((1, tn), lambda j: (0, j)),
        ],
        out_specs=pl.BlockSpec((rows, tn), lambda j: (0, j)),
        out_shape=jax.ShapeDtypeStruct((rows, n), F32),
        compiler_params=_params(("parallel",)),
        name="adaln",
    )(c_all, ada_w, ada_b.reshape(1, n))


def _proj_kernel(x_ref, mod_ref, g_ref, *refs):
    n_out = len(refs) // 2
    w_refs, o_refs = refs[:n_out], refs[n_out:]
    x = x_ref[0]
    ms = jnp.mean(x * x, axis=-1, keepdims=True)
    y = x * lax.rsqrt(ms + NORM_EPS) * g_ref[...]
    shift = mod_ref[0, 0:1, :]
    scale = mod_ref[0, 1:2, :]
    h = (y * (1.0 + scale) + shift).astype(BF16)
    for w_ref, o_ref in zip(w_refs, o_refs):
        o_ref[0] = _dot(h, w_ref[...])


def _proj(x, mod, norm_g, weights, tm, per_sample_mod):
    b, l, d = x.shape
    mod_map = (lambda i, t: (i, 0, 0)) if per_sample_mod else (lambda i, t: (0, 0, 0))
    in_specs = [
        pl.BlockSpec((1, tm, d), lambda i, t: (i, t, 0)),
        pl.BlockSpec((1, 6, d), mod_map),
        pl.BlockSpec((1, d), lambda i, t: (0, 0)),
    ]
    out_specs, out_shapes = [], []
    for w in weights:
        n = w.shape[1]
        in_specs.append(pl.BlockSpec((d, n), lambda i, t: (0, 0)))
        out_specs.append(pl.BlockSpec((1, tm, n), lambda i, t: (i, t, 0)))
        out_shapes.append(jax.ShapeDtypeStruct((b, l, n), F32))
    return pl.pallas_call(
        _proj_kernel,
        grid=(b, l // tm),
        in_specs=in_specs,
        out_specs=out_specs,
        out_shape=out_shapes,
        compiler_params=_params(("parallel", "parallel")),
        name="proj",
    )(x, mod, norm_g.reshape(1, d), *weights)


def _wkv_kernel(zc_ref, zp_ref, zn_ref, zl_ref, cw_ref, pv_ref, w2_ref, a2_ref, s0_ref,
                y_ref, bo_ref, s1_ref, s_ref, *, n_chunks):
    t = WKV_CHUNK
    rdim = y_ref.shape[-1]
    n_stack = rdim // MXU_DIM
    d = pl.program_id(1)
    j = pl.program_id(2)
    rev = d == 1
    order = 1 - 2 * d
    chunk = jnp.where(rev, n_chunks - 1 - j, j)

    @pl.when(j == 0)
    def _():
        s_ref[...] = s0_ref[0, 0]

    zm = zc_ref[0]
    row = lax.broadcasted_iota(jnp.int32, zm.shape, 0)
    prev_row = jnp.where(chunk > 0, zp_ref[0, 7:8, :], 0.0)
    next_row = jnp.where(chunk < n_chunks - 1, zn_ref[0, 0:1, :], 0.0)
    z_up = jnp.where(row == 0, prev_row, pltpu.roll(zm, 1, 0))
    z_dn = jnp.where(row == t - 1, next_row, pltpu.roll(zm, t - 1, 0))
    rkv = cw_ref[0:1, :] * z_up + cw_ref[1:2, :] * zm + cw_ref[2:3, :] * z_dn
    r = rkv[:, 0:rdim]
    k = rkv[:, rdim:2 * rdim]
    v = rkv[:, 2 * rdim:3 * rdim]

    w0 = pv_ref[0, 0:1, :]
    a0 = pv_ref[0, 1:2, :]
    kk_p = pv_ref[0, 2:3, :]
    ka_p = pv_ref[0, 3:4, :]
    rk_p = pv_ref[0, 4:5, :]

    wd = zl_ref[0, :, 0:LANES]
    ad = zl_ref[0, :, LANES:2 * LANES]
    wl = w0 + _dot(jnp.tanh(wd).astype(BF16), w2_ref[0])
    neg = -wl
    softplus = jnp.maximum(neg, 0.0) + jnp.log(1.0 + jnp.exp(-jnp.abs(neg)))
    lw = -jnp.exp(-softplus - 0.5)
    asig = _sigmoid(a0 + _dot(ad.astype(BF16), a2_ref[0]))

    ones_bd = _head_ones(MXU_DIM)
    kk = k * kk_p
    k_mod = k * (1.0 + (asig - 1.0) * ka_p)
    rkk = r * k_mod * rk_p

    tr = lax.broadcasted_iota(jnp.int32, (t, t), 0)
    tc = lax.broadcasted_iota(jnp.int32, (t, t), 1)
    tri = jnp.where((tr - tc) * order >= 0, 1.0, 0.0).astype(BF16)
    cum = _dot_split_rhs(tri, lw, 3)
    c_end = jnp.where(rev, cum[0:1, :], cum[t - 1:t, :])
    rho = 0.5 * c_end
    e_in = jnp.exp(cum - rho)
    e_out = jnp.exp(rho - cum)
    e_ex = jnp.exp(cum - lw - rho)
    e_rho = jnp.exp(rho)
    e_end = jnp.exp(c_end)

    sr = lax.broadcasted_iota(jnp.int32, (MXU_DIM, MXU_DIM), 0)
    sc = lax.broadcasted_iota(jnp.int32, (MXU_DIM, MXU_DIM), 1)
    block_mask = (sr >> HEAD_SHIFT) == (sc >> HEAD_SHIFT)
    ahead = (sr - sc) * order
    strict = ahead > 0
    incl = ahead >= 0
    eye = jnp.where(sr == sc, 1.0, 0.0)

    def stack(x):
        xs = jnp.concatenate([x] * HEADS_PER_STACK, axis=0)
        return jnp.where(block_mask, xs, 0.0).astype(BF16)

    for q in range(n_stack):
        sl = slice(q * MXU_DIM, (q + 1) * MXU_DIM)
        kk_q = kk[:, sl]
        ss = _head_sum(kk_q * kk_q, ones_bd)
        kkn = kk_q * lax.rsqrt(jnp.maximum(ss, 1e-24))
        a_vec = -kkn
        b_vec = kkn * asig[:, sl]
        bo_ref[0, 0, :, sl] = _head_sum(rkk[:, sl], ones_bd) * v[:, sl]

        er = e_rho[:, sl]
        a_t = a_vec * e_ex[:, sl]
        r_t = r[:, sl] * e_in[:, sl]
        b_t = b_vec * e_out[:, sl]
        k_t = k_mod[:, sl] * e_out[:, sl]
        a_st, r_st, b_st, k_st = stack(a_t), stack(r_t), stack(b_t), stack(k_t)
        a0_st, r0_st = stack(a_t * er), stack(r_t * er)
        bh_st, kh_st = stack(b_t * er), stack(k_t * er)
        v_st = stack(v[:, sl])

        n_mat = jnp.where(strict, _dot_nt(a_st, b_st), 0.0)
        a_ak = jnp.where(strict, _dot_nt(a_st, k_st), 0.0).astype(BF16)
        a_rb = jnp.where(incl, _dot_nt(r_st, b_st), 0.0).astype(BF16)
        a_rk = jnp.where(incl, _dot_nt(r_st, k_st), 0.0).astype(BF16)

        p = n_mat.astype(BF16)
        inv = eye + n_mat
        steps = t.bit_length() - 2
        for _ in range(steps):
            p32 = _dot(p, p)
            p = p32.astype(BF16)
            inv = inv + _dot(inv.astype(BF16), p)
        inv = inv.astype(BF16)

        s_q = s_ref[q]
        s_b = s_q.astype(BF16)
        x_mat = _dot_nt(a0_st, s_b) + _dot(a_ak, v_st)
        z_mat = _dot(inv, x_mat.astype(BF16))
        z_b = z_mat.astype(BF16)
        y_mat = _dot_nt(r0_st, s_b) + _dot(a_rb, z_b) + _dot(a_rk, v_st)
        s_ref[q] = s_q * e_end[:, sl] + _dot_tn(z_b, bh_st) + _dot_tn(v_st, kh_st)
        y_q = y_mat[0:t]
        for hh in range(1, HEADS_PER_STACK):
            y_q = y_q + y_mat[hh * t:(hh + 1) * t]
        y_ref[0, 0, :, sl] = y_q

    @pl.when(j == n_chunks - 1)
    def _():
        s1_ref[0, 0] = s_ref[...]


def _wkv(z_rkv, z_lora, conv_w8, pvec, w2_pad, a2_pad, s0):
    b, l, c3 = z_rkv.shape
    rdim = c3 // 3
    t = WKV_CHUNK
    nc = l // t
    n_stack = rdim // MXU_DIM
    hb = t // 8

    def cidx(dd, jj):
        return jnp.where(dd == 1, nc - 1 - jj, jj)

    in_specs = [
        pl.BlockSpec((1, t, c3), lambda i, dd, jj: (i, cidx(dd, jj), 0)),
        pl.BlockSpec((1, 8, c3), lambda i, dd, jj: (i, jnp.maximum(cidx(dd, jj) * hb - 1, 0), 0)),
        pl.BlockSpec((1, 8, c3), lambda i, dd, jj: (i, jnp.minimum((cidx(dd, jj) + 1) * hb, l // 8 - 1), 0)),
        pl.BlockSpec((1, t, 2 * LANES), lambda i, dd, jj: (i, cidx(dd, jj), 0)),
        pl.BlockSpec((8, c3), lambda i, dd, jj: (0, 0)),
        pl.BlockSpec((1, 8, rdim), lambda i, dd, jj: (dd, 0, 0)),
        pl.BlockSpec((1, LANES, rdim), lambda i, dd, jj: (dd, 0, 0)),
        pl.BlockSpec((1, LANES, rdim), lambda i, dd, jj: (dd, 0, 0)),
        pl.BlockSpec((1, 1, n_stack, MXU_DIM, MXU_DIM), lambda i, dd, jj: (i, dd, 0, 0, 0)),
    ]
    out_specs = [
        pl.BlockSpec((1, 1, t, rdim), lambda i, dd, jj: (i, dd, cidx(dd, jj), 0)),
        pl.BlockSpec((1, 1, t, rdim), lambda i, dd, jj: (i, dd, cidx(dd, jj), 0)),
        pl.BlockSpec((1, 1, n_stack, MXU_DIM, MXU_DIM), lambda i, dd, jj: (i, dd, 0, 0, 0)),
    ]
    out_shapes = [
        jax.ShapeDtypeStruct((b, 2, l, rdim), F32),
        jax.ShapeDtypeStruct((b, 2, l, rdim), F32),
        jax.ShapeDtypeStruct((b, 2, n_stack, MXU_DIM, MXU_DIM), F32),
    ]
    return pl.pallas_call(
        functools.partial(_wkv_kernel, n_chunks=nc),
        grid=(b, 2, nc),
        in_specs=in_specs,
        out_specs=out_specs,
        out_shape=out_shapes,
        scratch_shapes=[pltpu.VMEM((n_stack, MXU_DIM, MXU_DIM), F32)],
        compiler_params=_params(("parallel", "parallel", "arbitrary")),
        name="wkv",
    )(z_rkv, z_rkv, z_rkv, z_lora, conv_w8, pvec, w2_pad, a2_pad, s0)


def _rope(x, cos, sin):
    lane = lax.broadcasted_iota(jnp.int32, x.shape, 1)
    first = (lane & 31) < 16
    partner = jnp.where(first, pltpu.roll(x, LANES - 16, 1), pltpu.roll(x, 16, 1))
    return x * cos + partner * sin


def _head_rmsnorm(x, g, ones_bd):
    ss = _head_sum(x * x, ones_bd)
    return x * lax.rsqrt(ss * (1.0 / HEAD_DIM) + NORM_EPS) * g


def _attn_kernel(q_ref, kv_ref, kvc_ref, cos_ref, sin_ref, qg_ref, kg_ref, sink_ref, o_ref,
                 kp_ref, vp_ref, kc_ref, vc_ref, *, seq_len, n_q_heads):
    blk = ATT_BLOCK
    i = pl.program_id(1)
    ones_bd = _head_ones(LANES)
    kg = kg_ref[...]
    n_kv = kv_ref.shape[-1] // (2 * HEAD_DIM)
    kw = n_kv * HEAD_DIM
    group = n_q_heads // n_kv
    lc = kvc_ref.shape[1]

    @pl.when(i == 0)
    def _():
        zeros = jnp.zeros((blk, kw), BF16)
        kp_ref[0:blk, :] = zeros
        vp_ref[0:blk, :] = zeros
        kp_ref[blk + seq_len:2 * blk + seq_len, :] = zeros
        vp_ref[blk + seq_len:2 * blk + seq_len, :] = zeros
        kk = _head_rmsnorm(kv_ref[0, :, 0:kw], kg, ones_bd)
        kp_ref[blk:blk + seq_len, :] = _rope(kk, cos_ref[...], sin_ref[...]).astype(BF16)
        vp_ref[blk:blk + seq_len, :] = kv_ref[0, :, kw:2 * kw].astype(BF16)
        kc_ref[...] = _head_rmsnorm(kvc_ref[0, :, 0:kw], kg, ones_bd).astype(BF16)
        vc_ref[...] = kvc_ref[0, :, kw:2 * kw].astype(BF16)

    start = pl.multiple_of(i * blk, blk)
    cos = cos_ref[pl.ds(start, blk), :]
    sin = sin_ref[pl.ds(start, blk), :]
    k_win = kp_ref[pl.ds(start, 3 * blk), :]
    v_win = vp_ref[pl.ds(start, 3 * blk), :]
    k_ctx = kc_ref[...]
    v_ctx = vc_ref[...]

    lane = lax.broadcasted_iota(jnp.int32, (blk, LANES), 1)
    low_half = lane < HEAD_DIM
    high_half = lane >= HEAD_DIM

    q_heads = []
    for m in range(n_q_heads // 2):
        slab = q_ref[0, :, m * LANES:(m + 1) * LANES]
        slab = _rope(_head_rmsnorm(slab, qg_ref[...], ones_bd), cos, sin)
        q_heads.append(slab)

    rows = group * blk
    qr = lax.broadcasted_iota(jnp.int32, (rows, 3 * blk), 0) & (blk - 1)
    kc_i = lax.broadcasted_iota(jnp.int32, (rows, 3 * blk), 1)
    k_pos = start - blk + kc_i
    delta = kc_i - qr
    allowed = (delta >= 0) & (delta <= 2 * WINDOW) & (k_pos >= 0) & (k_pos < seq_len)
    scale = HEAD_DIM ** -0.5

    out_heads = [None] * n_q_heads
    for h in range(n_kv):
        kv_low = (h % 2) == 0
        parts = []
        for g in range(group):
            a = h * group + g
            slab = q_heads[a // 2]
            q_low = (a % 2) == 0
            if q_low != kv_low:
                slab = pltpu.roll(slab, HEAD_DIM, 1)
            parts.append(jnp.where(low_half if kv_low else high_half, slab, 0.0))
        qg = jnp.concatenate(parts, axis=0).astype(BF16)
        kslab = h // 2
        kw_h = k_win[:, kslab * LANES:(kslab + 1) * LANES]
        vw_h = v_win[:, kslab * LANES:(kslab + 1) * LANES]
        kc_h = k_ctx[:, kslab * LANES:(kslab + 1) * LANES]
        vc_h = v_ctx[:, kslab * LANES:(kslab + 1) * LANES]
        s_win = jnp.where(allowed, _dot_nt(qg, kw_h) * scale, MASK_VALUE)
        s_ctx = _dot_nt(qg, kc_h) * scale
        sink = jnp.concatenate(
            [jnp.full((blk, 1), sink_ref[h * group + g], F32) for g in range(group)], axis=0)
        mx = jnp.maximum(jnp.maximum(jnp.max(s_win, axis=-1, keepdims=True),
                                     jnp.max(s_ctx, axis=-1, keepdims=True)), sink)
        p_win = jnp.exp(s_win - mx)
        p_ctx = jnp.exp(s_ctx - mx)
        denom = (jnp.sum(p_win, axis=-1, keepdims=True) + jnp.sum(p_ctx, axis=-1, keepdims=True)
                 + jnp.exp(sink - mx))
        inv = 1.0 / denom
        o = _dot((p_win * inv).astype(BF16), vw_h) + _dot((p_ctx * inv).astype(BF16), vc_h)
        for g in range(group):
            a = h * group + g
            o_a = o[g * blk:(g + 1) * blk]
            if ((a % 2) == 0) != kv_low:
                o_a = pltpu.roll(o_a, HEAD_DIM, 1)
            out_heads[a] = o_a
    for m in range(n_q_heads // 2):
        o_ref[0, :, m * LANES:(m + 1) * LANES] = jnp.where(low_half, out_heads[2 * m], out_heads[2 * m + 1])


def _attn(q, kv, kv_ctx, cos, sin, q_g, k_g, sink):
    b, l, qd = q.shape
    lc = kv_ctx.shape[1]
    kvd = kv.shape[-1]
    kw = kvd // 2
    n_q_heads = qd // HEAD_DIM
    blk = ATT_BLOCK
    return pl.pallas_call(
        functools.partial(_attn_kernel, seq_len=l, n_q_heads=n_q_heads),
        grid=(b, l // blk),
        in_specs=[
            pl.BlockSpec((1, blk, qd), lambda i, t: (i, t, 0)),
            pl.BlockSpec((1, l, kvd), lambda i, t: (i, 0, 0)),
            pl.BlockSpec((1, lc, kvd), lambda i, t: (i, 0, 0)),
            pl.BlockSpec((l, LANES), lambda i, t: (0, 0)),
            pl.BlockSpec((l, LANES), lambda i, t: (0, 0)),
            pl.BlockSpec((1, LANES), lambda i, t: (0, 0)),
            pl.BlockSpec((1, LANES), lambda i, t: (0, 0)),
            pl.BlockSpec(memory_space=pltpu.SMEM),
        ],
        out_specs=pl.BlockSpec((1, blk, qd), lambda i, t: (i, t, 0)),
        out_shape=jax.ShapeDtypeStruct((b, l, qd), F32),
        scratch_shapes=[
            pltpu.VMEM((l + 2 * blk, kw), BF16),
            pltpu.VMEM((l + 2 * blk, kw), BF16),
            pltpu.VMEM((lc, kw), BF16),
            pltpu.VMEM((lc, kw), BF16),
        ],
        compiler_params=_params(("parallel", "arbitrary")),
        name="attn",
    )(q, kv, kv_ctx, cos, sin, q_g, k_g, sink)


def _rope_tables(seq_len):
    tpos = jnp.arange(seq_len, dtype=jnp.int32)
    row = (tpos // GRID_W).astype(F32)
    col = (tpos % GRID_W).astype(F32)
    dim = jnp.arange(HEAD_DIM, dtype=jnp.int32)
    half = HEAD_DIM // 4
    inv_freq = ROPE_BASE ** (-(dim % half).astype(F32) / half)
    pos = jnp.where((dim // (HEAD_DIM // 2))[None, :] == 0, row[:, None], col[:, None])
    ang = pos * inv_freq[None, :]
    sign = jnp.where((dim % (HEAD_DIM // 2)) < half, -1.0, 1.0)[None, :]
    cos = jnp.cos(ang)
    sin = jnp.sin(ang) * sign
    reps = LANES // HEAD_DIM
    return jnp.tile(cos, (1, reps)), jnp.tile(sin, (1, reps))


def _merge_kernel(y_ref, bo_ref, gd_ref, at_ref, gates_ref, x_ref, mod_ref, lnw_ref, lnb_ref, g2_ref,
                  wbr_ref, wba_ref, wo_ref, n2_ref, rw_ref, x1_ref, h2_ref, aff_ref):
    d_model = x_ref.shape[-1]
    ones_bd = _head_ones(MXU_DIM)
    rdim = y_ref.shape[-1]
    ys = y_ref[0, 0] + y_ref[0, 1]
    bonus = bo_ref[0, 0] + bo_ref[0, 1]
    cols = []
    for q in range(rdim // MXU_DIM):
        sl = slice(q * MXU_DIM, (q + 1) * MXU_DIM)
        yq = ys[:, sl]
        mean = _head_sum(yq, ones_bd) * (1.0 / HEAD_DIM)
        diff = yq - mean
        var = _head_sum(diff * diff, ones_bd) * (1.0 / HEAD_DIM)
        cols.append(diff * lax.rsqrt(var + LNX_EPS))
    yn = jnp.concatenate(cols, axis=1)
    y = yn * lnw_ref[...] + lnb_ref[...] + bonus
    g = _dot(_sigmoid(gd_ref[0]).astype(BF16), g2_ref[...])
    o_rwkv = (y * g).astype(BF16)
    br = _dot(o_rwkv, wbr_ref[...])
    ba = _dot(at_ref[0].astype(BF16), wba_ref[...])
    gates = gates_ref[0]
    merged = _sigmoid(gates[:, 0:d_model]) * br + _sigmoid(gates[:, d_model:2 * d_model]) * ba
    out = _dot(merged.astype(BF16), wo_ref[...])
    x1 = x_ref[0] + mod_ref[0, 2:3, :] * out
    x1_ref[0] = x1
    ms = jnp.mean(x1 * x1, axis=-1, keepdims=True)
    h2 = x1 * lax.rsqrt(ms + NORM_EPS) * n2_ref[...]
    h2 = h2 * (1.0 + mod_ref[0, 4:5, :]) + mod_ref[0, 3:4, :]
    h2_ref[0] = h2.astype(BF16)
    logits = lax.dot_general(rw_ref[...], h2, (((1,), (1,)), ((), ())),
                             precision=HIGHEST, preferred_element_type=F32)
    mx = jnp.max(logits, axis=0, keepdims=True)
    ex = jnp.exp(logits - mx)
    aff_ref[0] = ex / jnp.sum(ex, axis=0, keepdims=True)


def _merge(y, bonus, z_lora, at, gates, x, mod, lnx_w, lnx_b, g2, w_br_rwkv, w_br_attn, w_out,
           norm2_g, router_wt, tm):
    b, l, d = x.shape
    rdim = y.shape[-1]
    ne = router_wt.shape[0]
    const = lambda i, t: (0, 0)
    return pl.pallas_call(
        _merge_kernel,
        grid=(b, l // tm),
        in_specs=[
            pl.BlockSpec((1, 2, tm, rdim), lambda i, t: (i, 0, t, 0)),
            pl.BlockSpec((1, 2, tm, rdim), lambda i, t: (i, 0, t, 0)),
            pl.BlockSpec((1, tm, LANES), lambda i, t: (i, t, 2)),
            pl.BlockSpec((1, tm, at.shape[-1]), lambda i, t: (i, t, 0)),
            pl.BlockSpec((1, tm, 2 * d), lambda i, t: (i, t, 0)),
            pl.BlockSpec((1, tm, d), lambda i, t: (i, t, 0)),
            pl.BlockSpec((1, 6, d), lambda i, t: (i, 0, 0)),
            pl.BlockSpec((1, rdim), const),
            pl.BlockSpec((1, rdim), const),
            pl.BlockSpec(g2.shape, const),
            pl.BlockSpec(w_br_rwkv.shape, const),
            pl.BlockSpec(w_br_attn.shape, const),
            pl.BlockSpec(w_out.shape, const),
            pl.BlockSpec((1, d), const),
            pl.BlockSpec(router_wt.shape, const),
        ],
        out_specs=[
            pl.BlockSpec((1, tm, d), lambda i, t: (i, t, 0)),
            pl.BlockSpec((1, tm, d), lambda i, t: (i, t, 0)),
            pl.BlockSpec((1, ne, tm), lambda i, t: (i, 0, t)),
        ],
        out_shape=[
            jax.ShapeDtypeStruct((b, l, d), F32),
            jax.ShapeDtypeStruct((b, l, d), BF16),
            jax.ShapeDtypeStruct((b, ne, l), F32),
        ],
        compiler_params=_params(("parallel", "parallel")),
        name="merge",
    )(y, bonus, z_lora, at, gates, x, mod, lnx_w.reshape(1, rdim), lnx_b.reshape(1, rdim), g2,
      w_br_rwkv, w_br_attn, w_out, norm2_g.reshape(1, d), router_wt)


def _prefix_count(mask_f, tri_excl):
    rows, l = mask_f.shape
    running = jnp.zeros((rows, 1), F32)
    pieces = []
    for blk in range(l // LANES):
        m = mask_f[:, blk * LANES:(blk + 1) * LANES]
        pieces.append(_dot(m.astype(BF16), tri_excl) + running)
        running = running + jnp.sum(m, axis=1, keepdims=True)
    return jnp.concatenate(pieces, axis=1)


def _route_kernel(aff_ref, sel_ref, *, cap):
    aff = aff_ref[0]
    bits = lax.bitcast_convert_type(aff, jnp.int32)
    thr = jnp.zeros((aff.shape[0], 1), jnp.int32)
    for bit in range(30, -1, -1):
        cand = thr | (1 << bit)
        cnt = jnp.sum(jnp.where(bits >= cand, 1.0, 0.0), axis=1, keepdims=True)
        thr = jnp.where(cnt >= cap, cand, thr)
    r = lax.broadcasted_iota(jnp.int32, (LANES, LANES), 0)
    c = lax.broadcasted_iota(jnp.int32, (LANES, LANES), 1)
    tri_excl = jnp.where(r < c, 1.0, 0.0).astype(BF16)
    gt = jnp.where(bits > thr, 1.0, 0.0)
    eq = jnp.where(bits == thr, 1.0, 0.0)
    need = cap - jnp.sum(gt, axis=1, keepdims=True)
    eq_rank = _prefix_count(eq, tri_excl)
    chosen = jnp.maximum(gt, jnp.where(eq_rank < need, eq, 0.0))
    rank = _prefix_count(chosen, tri_excl)
    sel_ref[0] = jnp.where(chosen > 0.5, rank, -1.0).astype(jnp.int32)


def _route(aff_t, cap):
    b, ne, l = aff_t.shape
    return pl.pallas_call(
        functools.partial(_route_kernel, cap=cap),
        grid=(b,),
        in_specs=[pl.BlockSpec((1, ne, l), lambda i: (i, 0, 0))],
        out_specs=pl.BlockSpec((1, ne, l), lambda i: (i, 0, 0)),
        out_shape=jax.ShapeDtypeStruct((b, ne, l), jnp.int32),
        compiler_params=_params(("parallel",)),
        name="route",
    )(aff_t)


def _moe_kernel(sel_ref, aff_ref, h_ref, wg_ref, wu_ref, wd_ref, o_ref, xe_ref, ye_ref, g_ref, *,
                cap, n_f, scatter_tile):
    e = pl.program_id(1)
    f = pl.program_id(2)
    l = h_ref.shape[1]

    @pl.when((e == 0) & (f == 0))
    def _():
        o_ref[...] = jnp.zeros(o_ref.shape, F32)

    @pl.when(f == 0)
    def _():
        sel_row = sel_ref[0, pl.ds(e, 1), :]
        slot = lax.broadcasted_iota(jnp.int32, (cap, l), 0)
        onehot = jnp.where(slot == sel_row, 1.0, 0.0).astype(BF16)
        g_ref[...] = onehot
        xe_ref[...] = _dot(onehot, h_ref[0]).astype(BF16)
        ye_ref[...] = jnp.zeros(ye_ref.shape, F32)

    xe = xe_ref[...]
    hg = _dot(xe, wg_ref[0])
    hu = _dot(xe, wu_ref[0])
    act = (hg * _sigmoid(hg) * hu).astype(BF16)
    ye_ref[...] += _dot(act, wd_ref[0])

    @pl.when(f == n_f - 1)
    def _():
        sel_row = sel_ref[0, pl.ds(e, 1), :]
        aff_row = aff_ref[0, pl.ds(e, 1), :]
        slot = lax.broadcasted_iota(jnp.int32, (cap, l), 0)
        val = jnp.sum(jnp.where(slot == sel_row, aff_row, 0.0), axis=1, keepdims=True)
        yw = (ye_ref[...] * val).astype(BF16)
        for lt in range(l // scatter_tile):
            sl = slice(lt * scatter_tile, (lt + 1) * scatter_tile)
            o_ref[0, sl, :] += _dot_tn(g_ref[:, sl], yw)


def _moe(sel, aff_t, h2, wg, wu, wd, cap, n_f):
    b, l, d = h2.shape
    ne = sel.shape[1]
    ff = wg.shape[-1]
    fc = ff // n_f
    scatter_tile = min(l, 512)
    return pl.pallas_call(
        functools.partial(_moe_kernel, cap=cap, n_f=n_f, scatter_tile=scatter_tile),
        grid=(b, ne, n_f),
        in_specs=[
            pl.BlockSpec((1, ne, l), lambda i, e, f: (i, 0, 0)),
            pl.BlockSpec((1, ne, l), lambda i, e, f: (i, 0, 0)),
            pl.BlockSpec((1, l, d), lambda i, e, f: (i, 0, 0)),
            pl.BlockSpec((1, d, fc), lambda i, e, f: (e, 0, f)),
            pl.BlockSpec((1, d, fc), lambda i, e, f: (e, 0, f)),
            pl.BlockSpec((1, fc, d), lambda i, e, f: (e, f, 0)),
        ],
        out_specs=pl.BlockSpec((1, l, d), lambda i, e, f: (i, 0, 0)),
        out_shape=jax.ShapeDtypeStruct((b, l, d), F32),
        scratch_shapes=[
            pltpu.VMEM((cap, d), BF16),
            pltpu.VMEM((cap, d), F32),
            pltpu.VMEM((cap, l), BF16),
        ],
        compiler_params=_params(("parallel", "arbitrary", "arbitrary")),
        name="moe",
    )(sel, aff_t, h2, wg, wu, wd)


def _final_kernel(x1_ref, moe_ref, mod_ref, o_ref):
    o_ref[0] = x1_ref[0] + mod_ref[0, 5:6, :] * moe_ref[0]


def _final(x1, moe, mod, tm):
    b, l, d = x1.shape
    tok = pl.BlockSpec((1, tm, d), lambda i, t: (i, t, 0))
    return pl.pallas_call(
        _final_kernel,
        grid=(b, l // tm),
        in_specs=[tok, tok, pl.BlockSpec((1, 6, d), lambda i, t: (i, 0, 0))],
        out_specs=tok,
        out_shape=jax.ShapeDtypeStruct((b, l, d), F32),
        compiler_params=_params(("parallel", "parallel")),
        name="final",
    )(x1, moe, mod)


def _pad_rows(w, row0, rows):
    return jnp.zeros((rows, w.shape[1]), w.dtype).at[row0:row0 + w.shape[0]].set(w)


def _layer(x, ctx, mod_lat, mod_ctx, p):
    b, l, d = x.shape
    lc = ctx.shape[1]
    rdim = p["rwkv_w0"].shape[-1]
    dl = p["rwkv_w2"].shape[-2]
    al = p["rwkv_a2"].shape[-2]
    gl = p["rwkv_g2"].shape[0]
    att_dim = p["w_br_attn"].shape[0]
    kv_dim = (p["w_in"].shape[1] - 3 * rdim - 2 * dl - 2 * al - gl - att_dim - 2 * d) // 2
    assert 2 * dl == LANES and 2 * al == LANES and gl == LANES

    w_in = p["w_in"].astype(BF16)
    o = 0
    w_rkv = w_in[:, o:o + 3 * rdim]; o += 3 * rdim
    w_lora = w_in[:, o:o + 2 * dl + 2 * al + gl]; o += 2 * dl + 2 * al + gl
    w_q = w_in[:, o:o + att_dim]; o += att_dim
    w_kv = w_in[:, o:o + 2 * kv_dim]; o += 2 * kv_dim
    w_gates = w_in[:, o:o + 2 * d]

    tm = min(l, 512)
    z_rkv, z_lora, q, kv, gates = _proj(x, mod_lat, p["norm1_g"], [w_rkv, w_lora, w_q, w_kv, w_gates],
                                        min(l, 256), True)
    tmc = min(lc, 256)
    zc_rkv, zc_lora, kv_c = _proj(ctx, mod_ctx, p["norm1_g"], [w_rkv, w_lora, w_kv], tmc, False)

    conv_w8 = _pad_rows(p["rwkv_conv"], 0, 8)
    zeros3 = jnp.zeros((3, rdim), F32)
    pvec = jnp.stack([
        jnp.concatenate([p["rwkv_w0"][dd][None], p["rwkv_a0"][dd][None], p["rwkv_kk"][dd][None],
                         p["rwkv_ka"][dd][None], p["rwkv_rk"][dd].reshape(1, rdim), zeros3], axis=0)
        for dd in range(2)])
    w2_pad = jnp.stack([_pad_rows(p["rwkv_w2"][dd], dd * dl, LANES) for dd in range(2)]).astype(BF16)
    a2_pad = jnp.stack([_pad_rows(p["rwkv_a2"][dd], dd * al, LANES) for dd in range(2)]).astype(BF16)
    n_stack = rdim // MXU_DIM
    s0 = jnp.zeros((b, 2, n_stack, MXU_DIM, MXU_DIM), F32)
    _, _, s_ctx = _wkv(zc_rkv, zc_lora, conv_w8, pvec, w2_pad, a2_pad, s0)
    y, bonus, _ = _wkv(z_rkv, z_lora, conv_w8, pvec, w2_pad, a2_pad, s_ctx)

    cos, sin = _rope_tables(l)
    reps = LANES // HEAD_DIM
    q_g = jnp.tile(p["q_norm_g"], reps).reshape(1, LANES)
    k_g = jnp.tile(p["k_norm_g"], reps).reshape(1, LANES)
    at = _attn(q, kv, kv_c, cos, sin, q_g, k_g, p["attn_sink"])

    x1, h2, aff_t = _merge(y, bonus, z_lora, at, gates, x, mod_lat, p["lnx_w"], p["lnx_b"],
                           p["rwkv_g2"].astype(BF16), p["w_br_rwkv"].astype(BF16),
                           p["w_br_attn"].astype(BF16), p["w_out"].astype(BF16), p["norm2_g"],
                           p["router_w"].T, tm)
    ne = p["router_w"].shape[1]
    cap = EC_FACTOR * l // ne
    sel = _route(aff_t, cap)
    n_f = 2 if p["exp_w_gate"].shape[-1] % (2 * MXU_DIM) == 0 else 1
    moe = _moe(sel, aff_t, h2, p["exp_w_gate"].astype(BF16), p["exp_w_up"].astype(BF16),
               p["exp_w_down"].astype(BF16), cap, n_f)
    return _final(x1, moe, mod_lat, tm)


def kernel(x, c, ctx, c_ctx, ada_w, ada_b, norm1_g, norm2_g, w_in, rwkv_conv, rwkv_w0, rwkv_w2, rwkv_a0, rwkv_a2, rwkv_kk, rwkv_ka, rwkv_rk, rwkv_g2, lnx_w, lnx_b, q_norm_g, k_norm_g, attn_sink, w_br_rwkv, w_br_attn, w_out, router_w, exp_w_gate, exp_w_up, exp_w_down):
    b, l, d = x.shape
    depth = ada_w.shape[0]
    assert depth == 1, "the context stream is only read, never advanced, for a single layer"
    rows = -(-(b + 1) // 8) * 8
    c_all = jnp.zeros((rows, d), F32).at[:b].set(c).at[b].set(c_ctx)
    names = ["ada_w", "ada_b", "norm1_g", "norm2_g", "w_in", "rwkv_conv", "rwkv_w0", "rwkv_w2", "rwkv_a0",
             "rwkv_a2", "rwkv_kk", "rwkv_ka", "rwkv_rk", "rwkv_g2", "lnx_w", "lnx_b", "q_norm_g", "k_norm_g",
             "attn_sink", "w_br_rwkv", "w_br_attn", "w_out", "router_w", "exp_w_gate", "exp_w_up", "exp_w_down"]
    vals = [ada_w, ada_b, norm1_g, norm2_g, w_in, rwkv_conv, rwkv_w0, rwkv_w2, rwkv_a0, rwkv_a2, rwkv_kk,
            rwkv_ka, rwkv_rk, rwkv_g2, lnx_w, lnx_b, q_norm_g, k_norm_g, attn_sink, w_br_rwkv, w_br_attn,
            w_out, router_w, exp_w_gate, exp_w_up, exp_w_down]
    p = {n: v[0] for n, v in zip(names, vals)}
    mod = _adaln(c_all, p["ada_w"], p["ada_b"])
    mod_lat = mod[:b].reshape(b, 6, d)
    mod_ctx = mod[b:b + 1].reshape(1, 6, d)
    return _layer(x, ctx, mod_lat, mod_ctx, p)
```

```python
import functools

import jax
import jax.numpy as jnp
from jax import lax
from jax.experimental import pallas as pl
from jax.experimental.pallas import tpu as pltpu

F32 = jnp.float32
BF16 = jnp.bfloat16
HIGHEST = lax.Precision.HIGHEST

HEAD_DIM = 64
HEAD_SHIFT = 6
GRID_W = 64
WINDOW = 128
ATT_BLOCK = 128
ROPE_BASE = 10000.0
NORM_EPS = 1e-6
LNX_EPS = 64e-5
MASK_VALUE = -1e30
EC_FACTOR = 2

LANES = 128
MXU_DIM = 256
VMEM_LIMIT_BYTES = 56 * 1024 * 1024

WKV_CHUNK = 64
HEADS_PER_STACK = MXU_DIM // HEAD_DIM


def _dot(a, b):
    return jnp.dot(a, b, preferred_element_type=F32)


def _dot_nt(a, b):
    return lax.dot_general(a, b, (((1,), (1,)), ((), ())), preferred_element_type=F32)


def _dot_tn(a, b):
    return lax.dot_general(a, b, (((0,), (0,)), ((), ())), preferred_element_type=F32)


def _sigmoid(x):
    return 1.0 / (1.0 + jnp.exp(-x))


def _split_bf16(x, parts):
    out = []
    rem = x
    for _ in range(parts):
        hi = rem.astype(BF16)
        out.append(hi)
        rem = rem - hi.astype(F32)
    return out


def _dot_split_rhs(a_bf16, x, parts):
    acc = None
    for term in _split_bf16(x, parts):
        d = _dot(a_bf16, term)
        acc = d if acc is None else acc + d
    return acc


def _dot_split_lhs(x, b_bf16, parts):
    acc = None
    for term in _split_bf16(x, parts):
        d = _dot(term, b_bf16)
        acc = d if acc is None else acc + d
    return acc


def _head_ones(width):
    r = lax.broadcasted_iota(jnp.int32, (width, width), 0) >> HEAD_SHIFT
    c = lax.broadcasted_iota(jnp.int32, (width, width), 1) >> HEAD_SHIFT
    return jnp.where(r == c, 1.0, 0.0).astype(BF16)


def _head_sum(x, ones_bd):
    return _dot_split_lhs(x, ones_bd, 2)


def _params(semantics):
    return pltpu.CompilerParams(dimension_semantics=semantics, vmem_limit_bytes=VMEM_LIMIT_BYTES)


def _adaln_kernel(c_ref, w_ref, b_ref, o_ref):
    c = c_ref[...]
    s = c * _sigmoid(c)
    o_ref[...] = jnp.dot(s, w_ref[...], precision=HIGHEST, preferred_element_type=F32) + b_ref[...]


def _adaln(c_all, ada_w, ada_b):
    rows, d = c_all.shape
    n = ada_w.shape[1]
    tn = 512
    return pl.pallas_call(
        _adaln_kernel,
        grid=(n // tn,),
        in_specs=[
            pl.BlockSpec((rows, d), lambda j: (0, 0)),
            pl.BlockSpec((d, tn), lambda j: (0, j)),
            pl.BlockSpec((1, tn), lambda j: (0, j)),
        ],
        out_specs=pl.BlockSpec((rows, tn), lambda j: (0, j)),
        out_shape=jax.ShapeDtypeStruct((rows, n), F32),
        compiler_params=_params(("parallel",)),
        name="adaln",
    )(c_all, ada_w, ada_b.reshape(1, n))


def _proj_kernel(x_ref, mod_ref, g_ref, *refs):
    n_out = len(refs) // 2
    w_refs, o_refs = refs[:n_out], refs[n_out:]
    x = x_ref[0]
    ms = jnp.mean(x * x, axis=-1, keepdims=True)
    y = x * lax.rsqrt(ms + NORM_EPS) * g_ref[...]
    shift = mod_ref[0, 0:1, :]
    scale = mod_ref[0, 1:2, :]
    h = (y * (1.0 + scale) + shift).astype(BF16)
    for w_ref, o_ref in zip(w_refs, o_refs):
        o_ref[0] = _dot(h, w_ref[...])


def _proj(x, mod, norm_g, weights, tm, per_sample_mod):
    b, l, d = x.shape
    mod_map = (lambda i, t: (i, 0, 0)) if per_sample_mod else (lambda i, t: (0, 0, 0))
    in_specs = [
        pl.BlockSpec((1, tm, d), lambda i, t: (i, t, 0)),
        pl.BlockSpec((1, 6, d), mod_map),
        pl.BlockSpec((1, d), lambda i, t: (0, 0)),
    ]
    out_specs, out_shapes = [], []
    for w in weights:
        n = w.shape[1]
        in_specs.append(pl.BlockSpec((d, n), lambda i, t: (0, 0)))
        out_specs.append(pl.BlockSpec((1, tm, n), lambda i, t: (i, t, 0)))
        out_shapes.append(jax.ShapeDtypeStruct((b, l, n), F32))
    return pl.pallas_call(
        _proj_kernel,
        grid=(b, l // tm),
        in_specs=in_specs,
        out_specs=out_specs,
        out_shape=out_shapes,
        compiler_params=_params(("parallel", "parallel")),
        name="proj",
    )(x, mod, norm_g.reshape(1, d), *weights)


def _wkv_kernel(zc_ref, zp_ref, zn_ref, zl_ref, cw_ref, pv_ref, w2_ref, a2_ref, s0_ref,
                y_ref, bo_ref, s1_ref, s_ref, *, n_chunks):
    t = WKV_CHUNK
    rdim = y_ref.shape[-1]
    n_stack = rdim // MXU_DIM
    d = pl.program_id(1)
    j = pl.program_id(2)
    rev = d == 1
    order = 1 - 2 * d
    chunk = jnp.where(rev, n_chunks - 1 - j, j)

    @pl.when(j == 0)
    def _():
        s_ref[...] = s0_ref[0, 0]

    zm = zc_ref[0]
    row = lax.broadcasted_iota(jnp.int32, zm.shape, 0)
    prev_row = jnp.where(chunk > 0, zp_ref[0, 7:8, :], 0.0)
    next_row = jnp.where(chunk < n_chunks - 1, zn_ref[0, 0:1, :], 0.0)
    z_up = jnp.where(row == 0, prev_row, pltpu.roll(zm, 1, 0))
    z_dn = jnp.where(row == t - 1, next_row, pltpu.roll(zm, t - 1, 0))
    rkv = cw_ref[0:1, :] * z_up + cw_ref[1:2, :] * zm + cw_ref[2:3, :] * z_dn
    r = rkv[:, 0:rdim]
    k = rkv[:, rdim:2 * rdim]
    v = rkv[:, 2 * rdim:3 * rdim]

    w0 = pv_ref[0, 0:1, :]
    a0 = pv_ref[0, 1:2, :]
    kk_p = pv_ref[0, 2:3, :]
    ka_p = pv_ref[0, 3:4, :]
    rk_p = pv_ref[0, 4:5, :]

    wd = zl_ref[0, :, 0:LANES]
    ad = zl_ref[0, :, LANES:2 * LANES]
    wl = w0 + _dot(jnp.tanh(wd).astype(BF16), w2_ref[0])
    neg = -wl
    softplus = jnp.maximum(neg, 0.0) + jnp.log(1.0 + jnp.exp(-jnp.abs(neg)))
    lw = -jnp.exp(-softplus - 0.5)
    asig = _sigmoid(a0 + _dot(ad.astype(BF16), a2_ref[0]))

    ones_bd = _head_ones(MXU_DIM)
    kk = k * kk_p
    k_mod = k * (1.0 + (asig - 1.0) * ka_p)
    rkk = r * k_mod * rk_p

    tr = lax.broadcasted_iota(jnp.int32, (t, t), 0)
    tc = lax.broadcasted_iota(jnp.int32, (t, t), 1)
    tri = jnp.where((tr - tc) * order >= 0, 1.0, 0.0).astype(BF16)
    cum = _dot_split_rhs(tri, lw, 3)
    c_end = jnp.where(rev, cum[0:1, :], cum[t - 1:t, :])
    rho = 0.5 * c_end
    e_in = jnp.exp(cum - rho)
    e_out = jnp.exp(rho - cum)
    e_ex = jnp.exp(cum - lw - rho)
    e_rho = jnp.exp(rho)
    e_end = jnp.exp(c_end)

    sr = lax.broadcasted_iota(jnp.int32, (MXU_DIM, MXU_DIM), 0)
    sc = lax.broadcasted_iota(jnp.int32, (MXU_DIM, MXU_DIM), 1)
    block_mask = (sr >> HEAD_SHIFT) == (sc >> HEAD_SHIFT)
    ahead = (sr - sc) * order
    strict = ahead > 0
    incl = ahead >= 0
    eye = jnp.where(sr == sc, 1.0, 0.0)

    def stack(x):
        xs = jnp.concatenate([x] * HEADS_PER_STACK, axis=0)
        return jnp.where(block_mask, xs, 0.0).astype(BF16)

    for q in range(n_stack):
        sl = slice(q * MXU_DIM, (q + 1) * MXU_DIM)
        kk_q = kk[:, sl]
        ss = _head_sum(kk_q * kk_q, ones_bd)
        kkn = kk_q * lax.rsqrt(jnp.maximum(ss, 1e-24))
        a_vec = -kkn
        b_vec = kkn * asig[:, sl]
        bo_ref[0, 0, :, sl] = _head_sum(rkk[:, sl], ones_bd) * v[:, sl]

        er = e_rho[:, sl]
        a_t = a_vec * e_ex[:, sl]
        r_t = r[:, sl] * e_in[:, sl]
        b_t = b_vec * e_out[:, sl]
        k_t = k_mod[:, sl] * e_out[:, sl]
        a_st, r_st, b_st, k_st = stack(a_t), stack(r_t), stack(b_t), stack(k_t)
        a0_st, r0_st = stack(a_t * er), stack(r_t * er)
        bh_st, kh_st = stack(b_t * er), stack(k_t * er)
        v_st = stack(v[:, sl])

        n_mat = jnp.where(strict, _dot_nt(a_st, b_st), 0.0)
        a_ak = jnp.where(strict, _dot_nt(a_st, k_st), 0.0).astype(BF16)
        a_rb = jnp.where(incl, _dot_nt(r_st, b_st), 0.0).astype(BF16)
        a_rk = jnp.where(incl, _dot_nt(r_st, k_st), 0.0).astype(BF16)

        p = n_mat.astype(BF16)
        inv = eye + n_mat
        steps = t.bit_length() - 2
        for _ in range(steps):
            p32 = _dot(p, p)
            p = p32.astype(BF16)
            inv = inv + _dot(inv.astype(BF16), p)
        inv = inv.astype(BF16)

        s_q = s_ref[q]
        s_b = s_q.astype(BF16)
        x_mat = _dot_nt(a0_st, s_b) + _dot(a_ak, v_st)
        z_mat = _dot(inv, x_mat.astype(BF16))
        z_b = z_mat.astype(BF16)
        y_mat = _dot_nt(r0_st, s_b) + _dot(a_rb, z_b) + _dot(a_rk, v_st)
        s_ref[q] = s_q * e_end[:, sl] + _dot_tn(z_b, bh_st) + _dot_tn(v_st, kh_st)
        y_q = y_mat[0:t]
        for hh in range(1, HEADS_PER_STACK):
            y_q = y_q + y_mat[hh * t:(hh + 1) * t]
        y_ref[0, 0, :, sl] = y_q

    @pl.when(j == n_chunks - 1)
    def _():
        s1_ref[0, 0] = s_ref[...]


def _wkv(z_rkv, z_lora, conv_w8, pvec, w2_pad, a2_pad, s0):
    b, l, c3 = z_rkv.shape
    rdim = c3 // 3
    t = WKV_CHUNK
    nc = l // t
    n_stack = rdim // MXU_DIM
    hb = t // 8

    def cidx(dd, jj):
        return jnp.where(dd == 1, nc - 1 - jj, jj)

    in_specs = [
        pl.BlockSpec((1, t, c3), lambda i, dd, jj: (i, cidx(dd, jj), 0)),
        pl.BlockSpec((1, 8, c3), lambda i, dd, jj: (i, jnp.maximum(cidx(dd, jj) * hb - 1, 0), 0)),
        pl.BlockSpec((1, 8, c3), lambda i, dd, jj: (i, jnp.minimum((cidx(dd, jj) + 1) * hb, l // 8 - 1), 0)),
        pl.BlockSpec((1, t, 2 * LANES), lambda i, dd, jj: (i, cidx(dd, jj), 0)),
        pl.BlockSpec((8, c3), lambda i, dd, jj: (0, 0)),
        pl.BlockSpec((1, 8, rdim), lambda i, dd, jj: (dd, 0, 0)),
        pl.BlockSpec((1, LANES, rdim), lambda i, dd, jj: (dd, 0, 0)),
        pl.BlockSpec((1, LANES, rdim), lambda i, dd, jj: (dd, 0, 0)),
        pl.BlockSpec((1, 1, n_stack, MXU_DIM, MXU_DIM), lambda i, dd, jj: (i, dd, 0, 0, 0)),
    ]
    out_specs = [
        pl.BlockSpec((1, 1, t, rdim), lambda i, dd, jj: (i, dd, cidx(dd, jj), 0)),
        pl.BlockSpec((1, 1, t, rdim), lambda i, dd, jj: (i, dd, cidx(dd, jj), 0)),
        pl.BlockSpec((1, 1, n_stack, MXU_DIM, MXU_DIM), lambda i, dd, jj: (i, dd, 0, 0, 0)),
    ]
    out_shapes = [
        jax.ShapeDtypeStruct((b, 2, l, rdim), F32),
        jax.ShapeDtypeStruct((b, 2, l, rdim), F32),
        jax.ShapeDtypeStruct((b, 2, n_stack, MXU_DIM, MXU_DIM), F32),
    ]
    return pl.pallas_call(
        functools.partial(_wkv_kernel, n_chunks=nc),
        grid=(b, 2, nc),
        in_specs=in_specs,
        out_specs=out_specs,
        out_shape=out_shapes,
        scratch_shapes=[pltpu.VMEM((n_stack, MXU_DIM, MXU_DIM), F32)],
        compiler_params=_params(("parallel", "parallel", "arbitrary")),
        name="wkv",
    )(z_rkv, z_rkv, z_rkv, z_lora, conv_w8, pvec, w2_pad, a2_pad, s0)


def _wkv_chunk_inputs(dd, chunk, n_chunks, zc_ref, zp_ref, zn_ref, zl_ref, cw_ref, pv_ref, w2_ref, a2_ref):
    t = WKV_CHUNK
    rdim = pv_ref.shape[-1]
    zm = zc_ref[0]
    row = lax.broadcasted_iota(jnp.int32, zm.shape, 0)
    prev_row = jnp.where(chunk > 0, zp_ref[0, 7:8, :], 0.0)
    next_row = jnp.where(chunk < n_chunks - 1, zn_ref[0, 0:1, :], 0.0)
    z_up = jnp.where(row == 0, prev_row, pltpu.roll(zm, 1, 0))
    z_dn = jnp.where(row == t - 1, next_row, pltpu.roll(zm, t - 1, 0))
    rkv = cw_ref[0:1, :] * z_up + cw_ref[1:2, :] * zm + cw_ref[2:3, :] * z_dn
    r = rkv[:, 0:rdim]
    k = rkv[:, rdim:2 * rdim]
    v = rkv[:, 2 * rdim:3 * rdim]
    wd = zl_ref[0, :, 0:LANES]
    ad = zl_ref[0, :, LANES:2 * LANES].astype(BF16)
    wl = pv_ref[dd, 0:1, :] + _dot(jnp.tanh(wd).astype(BF16), w2_ref[dd])
    neg = -wl
    softplus = jnp.maximum(neg, 0.0) + jnp.log(1.0 + jnp.exp(-jnp.abs(neg)))
    lw = -jnp.exp(-softplus - 0.5)
    asig = _sigmoid(pv_ref[dd, 1:2, :] + _dot(ad, a2_ref[dd]))
    return r, k, v, lw, asig, ad


def _wkv2_kernel(*refs, n_chunks, emit_y):
    t = WKV_CHUNK
    (zcf, zpf, znf, zlf, zcb, zpb, znb, zlb, cw_ref, pv_ref, w2_ref, a2_ref, s0_ref) = refs[:13]
    if emit_y:
        yf_ref, yb_ref, bo_ref, s1_ref, s_ref = refs[13:]
    else:
        s1_ref, s_ref = refs[13:]
    rdim = pv_ref.shape[-1]
    n_stack = rdim // MXU_DIM
    j = pl.program_id(1)

    @pl.when(j == 0)
    def _():
        s_ref[...] = s0_ref[0]

    ones_bd = _head_ones(MXU_DIM)
    sr = lax.broadcasted_iota(jnp.int32, (MXU_DIM, MXU_DIM), 0)
    sc = lax.broadcasted_iota(jnp.int32, (MXU_DIM, MXU_DIM), 1)
    stack_mask = jnp.where((sr >> HEAD_SHIFT) == (sc >> HEAD_SHIFT), 1.0, 0.0).astype(BF16)
    eye = jnp.where(sr == sc, 1.0, 0.0)
    tr = lax.broadcasted_iota(jnp.int32, (t, t), 0)
    tc = lax.broadcasted_iota(jnp.int32, (t, t), 1)

    def stack(x):
        xb = x.astype(BF16)
        return jnp.concatenate([xb] * HEADS_PER_STACK, axis=0) * stack_mask

    chains = []
    for dd in range(2):
        rev = dd == 1
        chunk = (n_chunks - 1 - j) if rev else j
        io = (zcb, zpb, znb, zlb) if rev else (zcf, zpf, znf, zlf)
        r, k, v, lw, asig, ad = _wkv_chunk_inputs(dd, chunk, n_chunks, *io, cw_ref, pv_ref, w2_ref, a2_ref)
        kk = k * pv_ref[dd, 2:3, :]
        k_mod = k * (1.0 + (asig - 1.0) * pv_ref[dd, 3:4, :])
        if emit_y and not rev:
            asig_o = _sigmoid(pv_ref[1, 1:2, :] + _dot(ad, a2_ref[1]))
            k_mod_o = k * (1.0 + (asig_o - 1.0) * pv_ref[1, 3:4, :])
            rkk = r * (k_mod * pv_ref[0, 4:5, :] + k_mod_o * pv_ref[1, 4:5, :])

        tri = jnp.where((tr <= tc) if rev else (tr >= tc), 1.0, 0.0).astype(BF16)
        cum = _dot_split_rhs(tri, lw, 3)
        c_end = cum[0:1, :] if rev else cum[t - 1:t, :]
        rho = 0.5 * c_end
        e_in = jnp.exp(cum - rho)
        e_out = jnp.exp(rho - cum)
        e_ex = jnp.exp(cum - lw - rho)
        e_rho = jnp.exp(rho)
        e_end = jnp.exp(c_end)
        strict = (sr < sc) if rev else (sr > sc)
        incl = (sr <= sc) if rev else (sr >= sc)

        for q in range(n_stack):
            sl = slice(q * MXU_DIM, (q + 1) * MXU_DIM)
            kk_q = kk[:, sl]
            ss = _head_sum(kk_q * kk_q, ones_bd)
            kkn = kk_q * lax.rsqrt(jnp.maximum(ss, 1e-24))
            ch = dict(dd=dd, q=q, sl=sl, rev=rev, e_rho=e_rho[:, sl], e_end=e_end[:, sl])
            ch["a_st"] = stack(-kkn * e_ex[:, sl])
            ch["b_st"] = stack(kkn * asig[:, sl] * e_out[:, sl])
            ch["k_st"] = stack(k_mod[:, sl] * e_out[:, sl])
            ch["v_st"] = stack(v[:, sl])
            n_mat = jnp.where(strict, _dot_nt(ch["a_st"], ch["b_st"]), 0.0)
            ch["a_ak"] = jnp.where(strict, _dot_nt(ch["a_st"], ch["k_st"]), 0.0).astype(BF16)
            ch["p"] = n_mat.astype(BF16)
            ch["inv"] = eye + n_mat
            if emit_y:
                ch["r_st"] = stack(r[:, sl] * e_in[:, sl])
                ch["a_rb"] = jnp.where(incl, _dot_nt(ch["r_st"], ch["b_st"]), 0.0).astype(BF16)
                ch["a_rk"] = jnp.where(incl, _dot_nt(ch["r_st"], ch["k_st"]), 0.0).astype(BF16)
                if not rev:
                    bo_ref[0, :, sl] = _head_sum(rkk[:, sl], ones_bd) * v[:, sl]
            chains.append(ch)

    for _ in range(t.bit_length() - 2):
        for ch in chains:
            ch["p"] = _dot(ch["p"], ch["p"]).astype(BF16)
        for ch in chains:
            ch["inv"] = ch["inv"] + _dot(ch["inv"].astype(BF16), ch["p"])
    for ch in chains:
        ch["s_q"] = s_ref[ch["dd"], ch["q"]]
        ch["s_rho"] = (ch["s_q"] * ch["e_rho"]).astype(BF16)
        ch["x"] = (_dot_nt(ch["a_st"], ch["s_rho"]) + _dot(ch["a_ak"], ch["v_st"])).astype(BF16)
    for ch in chains:
        ch["z"] = _dot(ch["inv"].astype(BF16), ch["x"]).astype(BF16)
    for ch in chains:
        upd = _dot_tn(ch["z"], ch["b_st"]) + _dot_tn(ch["v_st"], ch["k_st"])
        s_ref[ch["dd"], ch["q"]] = ch["s_q"] * ch["e_end"] + upd * ch["e_rho"]
    if emit_y:
        for ch in chains:
            y_mat = (_dot_nt(ch["r_st"], ch["s_rho"]) + _dot(ch["a_rb"], ch["z"])
                     + _dot(ch["a_rk"], ch["v_st"]))
            y_q = y_mat[0:t]
            for hh in range(1, HEADS_PER_STACK):
                y_q = y_q + y_mat[hh * t:(hh + 1) * t]
            (yb_ref if ch["rev"] else yf_ref)[0, :, ch["sl"]] = y_q

    @pl.when(j == n_chunks - 1)
    def _():
        s1_ref[0] = s_ref[...]


def _wkv2(z_rkv, z_lora, conv_w8, pvec, w2_pad, a2_pad, s0, emit_y):
    b, l, c3 = z_rkv.shape
    rdim = c3 // 3
    t = WKV_CHUNK
    nc = l // t
    n_stack = rdim // MXU_DIM
    hb = t // 8
    last8 = l // 8 - 1

    def chunk_specs(cidx):
        return [
            pl.BlockSpec((1, t, c3), lambda i, jj: (i, cidx(jj), 0)),
            pl.BlockSpec((1, 8, c3), lambda i, jj: (i, jnp.maximum(cidx(jj) * hb - 1, 0), 0)),
            pl.BlockSpec((1, 8, c3), lambda i, jj: (i, jnp.minimum((cidx(jj) + 1) * hb, last8), 0)),
            pl.BlockSpec((1, t, 2 * LANES), lambda i, jj: (i, cidx(jj), 0)),
        ]

    fwd = lambda jj: jj
    bwd = lambda jj: nc - 1 - jj
    state_spec = pl.BlockSpec((1, 2, n_stack, MXU_DIM, MXU_DIM), lambda i, jj: (i, 0, 0, 0, 0))
    in_specs = chunk_specs(fwd) + chunk_specs(bwd) + [
        pl.BlockSpec((8, c3), lambda i, jj: (0, 0)),
        pl.BlockSpec((2, 8, rdim), lambda i, jj: (0, 0, 0)),
        pl.BlockSpec((2, LANES, rdim), lambda i, jj: (0, 0, 0)),
        pl.BlockSpec((2, LANES, rdim), lambda i, jj: (0, 0, 0)),
        state_spec,
    ]
    out_specs, out_shapes = [], []
    if emit_y:
        out_specs += [
            pl.BlockSpec((1, t, rdim), lambda i, jj: (i, fwd(jj), 0)),
            pl.BlockSpec((1, t, rdim), lambda i, jj: (i, bwd(jj), 0)),
            pl.BlockSpec((1, t, rdim), lambda i, jj: (i, fwd(jj), 0)),
        ]
        out_shapes += [jax.ShapeDtypeStruct((b, l, rdim), F32)] * 3
    out_specs.append(state_spec)
    out_shapes.append(jax.ShapeDtypeStruct((b, 2, n_stack, MXU_DIM, MXU_DIM), F32))
    return pl.pallas_call(
        functools.partial(_wkv2_kernel, n_chunks=nc, emit_y=emit_y),
        grid=(b, nc),
        in_specs=in_specs,
        out_specs=out_specs,
        out_shape=out_shapes,
        scratch_shapes=[pltpu.VMEM((2, n_stack, MXU_DIM, MXU_DIM), F32)],
        compiler_params=_params(("parallel", "arbitrary")),
        name="wkv",
    )(z_rkv, z_rkv, z_rkv, z_lora, z_rkv, z_rkv, z_rkv, z_lora, conv_w8, pvec, w2_pad, a2_pad, s0)


_M_A, _M_B, _M_K, _M_V, _M_AK, _M_P, _M_R, _M_RB, _M_RK = range(9)


def _wkv3_kernel(*refs, n_chunks, emit_y):
    t = WKV_CHUNK
    (zcf, zpf, znf, zlf, zcb, zpb, znb, zlb, cw_ref, pv_ref, w2_ref, a2_ref, s0_ref) = refs[:13]
    if emit_y:
        yf_ref, yb_ref, bo_ref, s1_ref, s_ref, mats_ref, inv_ref, rows_ref = refs[13:]
    else:
        s1_ref, s_ref, mats_ref, inv_ref, rows_ref = refs[13:]
    rdim = pv_ref.shape[-1]
    n_stack = rdim // MXU_DIM
    n_chain = 2 * n_stack
    s = pl.program_id(1)
    valid = s >= 1

    @pl.when(s == 0)
    def _():
        s_ref[...] = s0_ref[0]
        mats_ref[...] = jnp.zeros(mats_ref.shape, BF16)
        inv_ref[...] = jnp.zeros(inv_ref.shape, F32)
        rows_ref[...] = jnp.zeros(rows_ref.shape, F32)

    ones_bd = _head_ones(MXU_DIM)
    sr = lax.broadcasted_iota(jnp.int32, (MXU_DIM, MXU_DIM), 0)
    sc = lax.broadcasted_iota(jnp.int32, (MXU_DIM, MXU_DIM), 1)
    stack_mask = jnp.where((sr >> HEAD_SHIFT) == (sc >> HEAD_SHIFT), 1.0, 0.0).astype(BF16)
    eye = jnp.where(sr == sc, 1.0, 0.0)
    tr = lax.broadcasted_iota(jnp.int32, (t, t), 0)
    tc = lax.broadcasted_iota(jnp.int32, (t, t), 1)

    def stack(x):
        xb = x.astype(BF16)
        return jnp.concatenate([xb] * HEADS_PER_STACK, axis=0) * stack_mask

    def prep_inputs(dd):
        rev = dd == 1
        chunk = jnp.maximum(n_chunks - 1 - s, 0) if rev else jnp.minimum(s, n_chunks - 1)
        io = (zcb, zpb, znb, zlb) if rev else (zcf, zpf, znf, zlf)
        r, k, v, lw, asig, ad = _wkv_chunk_inputs(dd, chunk, n_chunks, *io, cw_ref, pv_ref, w2_ref, a2_ref)
        pi = dict(dd=dd, rev=rev, r=r, k=k, v=v, asig=asig, ad=ad)
        pi["kk"] = k * pv_ref[dd, 2:3, :]
        pi["k_mod"] = k * (1.0 + (asig - 1.0) * pv_ref[dd, 3:4, :])
        tri = jnp.where((tr <= tc) if rev else (tr >= tc), 1.0, 0.0).astype(BF16)
        cum = _dot_split_rhs(tri, lw, 3)
        c_end = cum[0:1, :] if rev else cum[t - 1:t, :]
        rho = 0.5 * c_end
        pi["e_in"] = jnp.exp(cum - rho)
        pi["e_out"] = jnp.exp(rho - cum)
        pi["e_ex"] = jnp.exp(cum - lw - rho)
        pi["e_rho"] = jnp.exp(rho)
        pi["e_end"] = jnp.exp(c_end)
        return pi

    def prep_chains(pi):
        rev = pi["rev"]
        strict = (sr < sc) if rev else (sr > sc)
        incl = (sr <= sc) if rev else (sr >= sc)
        out = []
        for q in range(n_stack):
            sl = slice(q * MXU_DIM, (q + 1) * MXU_DIM)
            kk_q = pi["kk"][:, sl]
            ss = _head_sum(kk_q * kk_q, ones_bd)
            kkn = kk_q * lax.rsqrt(jnp.maximum(ss, 1e-24))
            m = {}
            m[_M_A] = stack(-kkn * pi["e_ex"][:, sl])
            m[_M_B] = stack(kkn * pi["asig"][:, sl] * pi["e_out"][:, sl])
            m[_M_K] = stack(pi["k_mod"][:, sl] * pi["e_out"][:, sl])
            m[_M_V] = stack(pi["v"][:, sl])
            n_mat = jnp.where(strict, _dot_nt(m[_M_A], m[_M_B]), 0.0)
            m[_M_AK] = jnp.where(strict, _dot_nt(m[_M_A], m[_M_K]), 0.0).astype(BF16)
            m[_M_P] = n_mat.astype(BF16)
            if emit_y:
                m[_M_R] = stack(pi["r"][:, sl] * pi["e_in"][:, sl])
                m[_M_RB] = jnp.where(incl, _dot_nt(m[_M_R], m[_M_B]), 0.0).astype(BF16)
                m[_M_RK] = jnp.where(incl, _dot_nt(m[_M_R], m[_M_K]), 0.0).astype(BF16)
            out.append(dict(c=pi["dd"] * n_stack + q, mats=m, inv=eye + n_mat,
                            e_rho=pi["e_rho"][:, sl], e_end=pi["e_end"][:, sl]))
        return out

    def bonus(pi):
        asig_o = _sigmoid(pv_ref[1, 1:2, :] + _dot(pi["ad"], a2_ref[1]))
        k_mod_o = pi["k"] * (1.0 + (asig_o - 1.0) * pv_ref[1, 3:4, :])
        rkk = pi["r"] * (pi["k_mod"] * pv_ref[0, 4:5, :] + k_mod_o * pv_ref[1, 4:5, :])
        for q in range(n_stack):
            sl = slice(q * MXU_DIM, (q + 1) * MXU_DIM)
            bo_ref[0, :, sl] = _head_sum(rkk[:, sl], ones_bd) * pi["v"][:, sl]

    p_cur = [mats_ref[c, _M_P] for c in range(n_chain)]
    inv_cur = [inv_ref[c] for c in range(n_chain)]

    def doubling_step():
        for c in range(n_chain):
            p_cur[c] = _dot(p_cur[c], p_cur[c]).astype(BF16)
        for c in range(n_chain):
            inv_cur[c] = inv_cur[c] + _dot(inv_cur[c].astype(BF16), p_cur[c])

    n_double = t.bit_length() - 2
    assert n_double == 5
    pi0 = prep_inputs(0)
    doubling_step()
    new_chains = prep_chains(pi0)
    doubling_step()
    pi1 = prep_inputs(1)
    doubling_step()
    new_chains += prep_chains(pi1)
    doubling_step()
    if emit_y:
        bonus(pi0)
    doubling_step()

    s_old, s_rho, x_b, z_b = [], [], [], []
    for c in range(n_chain):
        s_old.append(s_ref[c // n_stack, c % n_stack])
        s_rho.append((s_old[c] * rows_ref[c, 0:1, :]).astype(BF16))
        x_b.append((_dot_nt(mats_ref[c, _M_A], s_rho[c]) + _dot(mats_ref[c, _M_AK], mats_ref[c, _M_V])).astype(BF16))
    for c in range(n_chain):
        z_b.append(_dot(inv_cur[c].astype(BF16), x_b[c]).astype(BF16))
    for c in range(n_chain):
        upd = _dot_tn(z_b[c], mats_ref[c, _M_B]) + _dot_tn(mats_ref[c, _M_V], mats_ref[c, _M_K])
        s_new = s_old[c] * rows_ref[c, 1:2, :] + upd * rows_ref[c, 0:1, :]
        s_ref[c // n_stack, c % n_stack] = jnp.where(valid, s_new, s_old[c])
    if emit_y:
        for c in range(n_chain):
            y_mat = (_dot_nt(mats_ref[c, _M_R], s_rho[c]) + _dot(mats_ref[c, _M_RB], z_b[c])
                     + _dot(mats_ref[c, _M_RK], mats_ref[c, _M_V]))
            y_q = y_mat[0:t]
            for hh in range(1, HEADS_PER_STACK):
                y_q = y_q + y_mat[hh * t:(hh + 1) * t]
            sl = slice((c % n_stack) * MXU_DIM, (c % n_stack + 1) * MXU_DIM)
            (yb_ref if c >= n_stack else yf_ref)[0, :, sl] = y_q

    for ch in new_chains:
        c = ch["c"]
        for slot, val in ch["mats"].items():
            mats_ref[c, slot] = val
        inv_ref[c] = ch["inv"]
        rows_ref[c, 0:1, :] = ch["e_rho"]
        rows_ref[c, 1:2, :] = ch["e_end"]

    @pl.when(s == n_chunks)
    def _():
        s1_ref[0] = s_ref[...]


def _wkv3(z_rkv, z_lora, conv_w8, pvec, w2_pad, a2_pad, s0, emit_y):
    b, l, c3 = z_rkv.shape
    rdim = c3 // 3
    t = WKV_CHUNK
    nc = l // t
    n_stack = rdim // MXU_DIM
    hb = t // 8
    last8 = l // 8 - 1

    def chunk_specs(cidx):
        return [
            pl.BlockSpec((1, t, c3), lambda i, s: (i, cidx(s), 0)),
            pl.BlockSpec((1, 8, c3), lambda i, s: (i, jnp.maximum(cidx(s) * hb - 1, 0), 0)),
            pl.BlockSpec((1, 8, c3), lambda i, s: (i, jnp.minimum((cidx(s) + 1) * hb, last8), 0)),
            pl.BlockSpec((1, t, 2 * LANES), lambda i, s: (i, cidx(s), 0)),
        ]

    prep_f = lambda s: jnp.minimum(s, nc - 1)
    prep_b = lambda s: jnp.maximum(nc - 1 - s, 0)
    run_f = lambda s: jnp.maximum(s - 1, 0)
    run_b = lambda s: jnp.minimum(nc - s, nc - 1)
    state_spec = pl.BlockSpec((1, 2, n_stack, MXU_DIM, MXU_DIM), lambda i, s: (i, 0, 0, 0, 0))
    in_specs = chunk_specs(prep_f) + chunk_specs(prep_b) + [
        pl.BlockSpec((8, c3), lambda i, s: (0, 0)),
        pl.BlockSpec((2, 8, rdim), lambda i, s: (0, 0, 0)),
        pl.BlockSpec((2, LANES, rdim), lambda i, s: (0, 0, 0)),
        pl.BlockSpec((2, LANES, rdim), lambda i, s: (0, 0, 0)),
        state_spec,
    ]
    out_specs, out_shapes = [], []
    if emit_y:
        out_specs += [
            pl.BlockSpec((1, t, rdim), lambda i, s: (i, run_f(s), 0)),
            pl.BlockSpec((1, t, rdim), lambda i, s: (i, run_b(s), 0)),
            pl.BlockSpec((1, t, rdim), lambda i, s: (i, prep_f(s), 0)),
        ]
        out_shapes += [jax.ShapeDtypeStruct((b, l, rdim), F32)] * 3
    out_specs.append(state_spec)
    out_shapes.append(jax.ShapeDtypeStruct((b, 2, n_stack, MXU_DIM, MXU_DIM), F32))
    n_mats = 9 if emit_y else 6
    return pl.pallas_call(
        functools.partial(_wkv3_kernel, n_chunks=nc, emit_y=emit_y),
        grid=(b, nc + 1),
        in_specs=in_specs,
        out_specs=out_specs,
        out_shape=out_shapes,
        scratch_shapes=[
            pltpu.VMEM((2, n_stack, MXU_DIM, MXU_DIM), F32),
            pltpu.VMEM((2 * n_stack, n_mats, MXU_DIM, MXU_DIM), BF16),
            pltpu.VMEM((2 * n_stack, MXU_DIM, MXU_DIM), F32),
            pltpu.VMEM((2 * n_stack, 8, MXU_DIM), F32),
        ],
        compiler_params=_params(("parallel", "arbitrary")),
        name="wkv",
    )(z_rkv, z_rkv, z_rkv, z_lora, z_rkv, z_rkv, z_rkv, z_lora, conv_w8, pvec, w2_pad, a2_pad, s0)


def _rope(x, cos, sin):
    lane = lax.broadcasted_iota(jnp.int32, x.shape, 1)
    first = (lane & 31) < 16
    partner = jnp.where(first, pltpu.roll(x, LANES - 16, 1), pltpu.roll(x, 16, 1))
    return x * cos + partner * sin


def _head_rmsnorm(x, g, ones_bd):
    ss = _head_sum(x * x, ones_bd)
    return x * lax.rsqrt(ss * (1.0 / HEAD_DIM) + NORM_EPS) * g


def _attn_kernel(q_ref, kv_ref, kvc_ref, cos_ref, sin_ref, qg_ref, kg_ref, sink_ref, o_ref,
                 kp_ref, vp_ref, kc_ref, vc_ref, *, seq_len, n_q_heads):
    blk = ATT_BLOCK
    i = pl.program_id(1)
    ones_bd = _head_ones(LANES)
    kg = kg_ref[...]
    n_kv = kv_ref.shape[-1] // (2 * HEAD_DIM)
    kw = n_kv * HEAD_DIM
    group = n_q_heads // n_kv
    lc = kvc_ref.shape[1]

    @pl.when(i == 0)
    def _():
        zeros = jnp.zeros((blk, kw), BF16)
        kp_ref[0:blk, :] = zeros
        vp_ref[0:blk, :] = zeros
        kp_ref[blk + seq_len:2 * blk + seq_len, :] = zeros
        vp_ref[blk + seq_len:2 * blk + seq_len, :] = zeros
        kk = _head_rmsnorm(kv_ref[0, :, 0:kw], kg, ones_bd)
        kp_ref[blk:blk + seq_len, :] = _rope(kk, cos_ref[...], sin_ref[...]).astype(BF16)
        vp_ref[blk:blk + seq_len, :] = kv_ref[0, :, kw:2 * kw].astype(BF16)
        kc_ref[...] = _head_rmsnorm(kvc_ref[0, :, 0:kw], kg, ones_bd).astype(BF16)
        vc_ref[...] = kvc_ref[0, :, kw:2 * kw].astype(BF16)

    start = pl.multiple_of(i * blk, blk)
    cos = cos_ref[pl.ds(start, blk), :]
    sin = sin_ref[pl.ds(start, blk), :]
    k_win = kp_ref[pl.ds(start, 3 * blk), :]
    v_win = vp_ref[pl.ds(start, 3 * blk), :]
    k_ctx = kc_ref[...]
    v_ctx = vc_ref[...]

    lane = lax.broadcasted_iota(jnp.int32, (blk, LANES), 1)
    low_half = lane < HEAD_DIM
    high_half = lane >= HEAD_DIM

    scale = HEAD_DIM ** -0.5
    q_heads = []
    for m in range(n_q_heads // 2):
        slab = q_ref[0, :, m * LANES:(m + 1) * LANES]
        slab = _rope(_head_rmsnorm(slab, qg_ref[...], ones_bd), cos, sin) * scale
        q_heads.append(slab)

    qr = lax.broadcasted_iota(jnp.int32, (blk, 3 * blk), 0)
    kc_i = lax.broadcasted_iota(jnp.int32, (blk, 3 * blk), 1)
    lo = jnp.maximum(qr, blk - start)
    hi = jnp.minimum(qr + 2 * WINDOW, seq_len + blk - 1 - start)
    hidden = jnp.where(kc_i < lo, 1.0, jnp.where(kc_i > hi, 1.0, 0.0))
    hidden = jnp.concatenate([hidden] * group, axis=0)

    heads = []
    for h in range(n_kv):
        kv_low = (h % 2) == 0
        parts = []
        for g in range(group):
            a = h * group + g
            slab = q_heads[a // 2]
            if ((a % 2) == 0) != kv_low:
                slab = pltpu.roll(slab, HEAD_DIM, 1)
            parts.append(jnp.where(low_half if kv_low else high_half, slab, 0.0))
        sl = slice((h // 2) * LANES, (h // 2 + 1) * LANES)
        heads.append(dict(
            kv_low=kv_low, qg=jnp.concatenate(parts, axis=0).astype(BF16),
            k_all=jnp.concatenate([k_win[:, sl], k_ctx[:, sl]], axis=0),
            v_all=jnp.concatenate([v_win[:, sl], v_ctx[:, sl]], axis=0),
            sink=jnp.concatenate([jnp.full((blk, 1), sink_ref[h * group + g], F32) for g in range(group)],
                                 axis=0)))
    for hd in heads:
        s_all = _dot_nt(hd["qg"], hd["k_all"])
        hd["s"] = jnp.concatenate(
            [jnp.where(hidden > 0.5, MASK_VALUE, s_all[:, 0:3 * blk]), s_all[:, 3 * blk:]], axis=1)
    for hd in heads:
        hd["mx"] = jnp.maximum(jnp.max(hd["s"], axis=-1, keepdims=True), hd["sink"])
    for hd in heads:
        hd["p"] = jnp.exp(hd["s"] - hd["mx"])
    for hd in heads:
        hd["den"] = jnp.sum(hd["p"], axis=-1, keepdims=True) + jnp.exp(hd["sink"] - hd["mx"])
    for hd in heads:
        hd["o"] = _dot(hd["p"].astype(BF16), hd["v_all"]) * (1.0 / hd["den"])
    out_heads = [None] * n_q_heads
    for h, hd in enumerate(heads):
        for g in range(group):
            a = h * group + g
            o_a = hd["o"][g * blk:(g + 1) * blk]
            if ((a % 2) == 0) != hd["kv_low"]:
                o_a = pltpu.roll(o_a, HEAD_DIM, 1)
            out_heads[a] = o_a
    for m in range(n_q_heads // 2):
        o_ref[0, :, m * LANES:(m + 1) * LANES] = jnp.where(low_half, out_heads[2 * m], out_heads[2 * m + 1])


def _attn(q, kv, kv_ctx, cos, sin, q_g, k_g, sink):
    b, l, qd = q.shape
    lc = kv_ctx.shape[1]
    kvd = kv.shape[-1]
    kw = kvd // 2
    n_q_heads = qd // HEAD_DIM
    blk = ATT_BLOCK
    return pl.pallas_call(
        functools.partial(_attn_kernel, seq_len=l, n_q_heads=n_q_heads),
        grid=(b, l // blk),
        in_specs=[
            pl.BlockSpec((1, blk, qd), lambda i, t: (i, t, 0)),
            pl.BlockSpec((1, l, kvd), lambda i, t: (i, 0, 0)),
            pl.BlockSpec((1, lc, kvd), lambda i, t: (i, 0, 0)),
            pl.BlockSpec((l, LANES), lambda i, t: (0, 0)),
            pl.BlockSpec((l, LANES), lambda i, t: (0, 0)),
            pl.BlockSpec((1, LANES), lambda i, t: (0, 0)),
            pl.BlockSpec((1, LANES), lambda i, t: (0, 0)),
            pl.BlockSpec(memory_space=pltpu.SMEM),
        ],
        out_specs=pl.BlockSpec((1, blk, qd), lambda i, t: (i, t, 0)),
        out_shape=jax.ShapeDtypeStruct((b, l, qd), F32),
        scratch_shapes=[
            pltpu.VMEM((l + 2 * blk, kw), BF16),
            pltpu.VMEM((l + 2 * blk, kw), BF16),
            pltpu.VMEM((lc, kw), BF16),
            pltpu.VMEM((lc, kw), BF16),
        ],
        compiler_params=_params(("parallel", "arbitrary")),
        name="attn",
    )(q, kv, kv_ctx, cos, sin, q_g, k_g, sink)


def _rope_tables(seq_len):
    tpos = jnp.arange(seq_len, dtype=jnp.int32)
    row = (tpos // GRID_W).astype(F32)
    col = (tpos % GRID_W).astype(F32)
    dim = jnp.arange(HEAD_DIM, dtype=jnp.int32)
    half = HEAD_DIM // 4
    inv_freq = ROPE_BASE ** (-(dim % half).astype(F32) / half)
    pos = jnp.where((dim // (HEAD_DIM // 2))[None, :] == 0, row[:, None], col[:, None])
    ang = pos * inv_freq[None, :]
    sign = jnp.where((dim % (HEAD_DIM // 2)) < half, -1.0, 1.0)[None, :]
    cos = jnp.cos(ang)
    sin = jnp.sin(ang) * sign
    reps = LANES // HEAD_DIM
    return jnp.tile(cos, (1, reps)), jnp.tile(sin, (1, reps))


def _merge_kernel(yf_ref, yb_ref, bo_ref, gd_ref, at_ref, gates_ref, x_ref, mod_ref, lnw_ref, lnb_ref, g2_ref,
                  wbr_ref, wba_ref, wo_ref, n2_ref, rw_ref, x1_ref, h2_ref, aff_ref):
    d_model = x_ref.shape[-1]
    ones_bd = _head_ones(MXU_DIM)
    rdim = yf_ref.shape[-1]
    ys = yf_ref[0] + yb_ref[0]
    bonus = bo_ref[0]
    cols = []
    for q in range(rdim // MXU_DIM):
        sl = slice(q * MXU_DIM, (q + 1) * MXU_DIM)
        yq = ys[:, sl]
        mean = _head_sum(yq, ones_bd) * (1.0 / HEAD_DIM)
        diff = yq - mean
        var = _head_sum(diff * diff, ones_bd) * (1.0 / HEAD_DIM)
        cols.append(diff * lax.rsqrt(var + LNX_EPS))
    yn = jnp.concatenate(cols, axis=1)
    y = yn * lnw_ref[...] + lnb_ref[...] + bonus
    g = _dot(_sigmoid(gd_ref[0]).astype(BF16), g2_ref[...])
    o_rwkv = (y * g).astype(BF16)
    br = _dot(o_rwkv, wbr_ref[...])
    ba = _dot(at_ref[0].astype(BF16), wba_ref[...])
    gates = gates_ref[0]
    merged = _sigmoid(gates[:, 0:d_model]) * br + _sigmoid(gates[:, d_model:2 * d_model]) * ba
    out = _dot(merged.astype(BF16), wo_ref[...])
    x1 = x_ref[0] + mod_ref[0, 2:3, :] * out
    x1_ref[0] = x1
    ms = jnp.mean(x1 * x1, axis=-1, keepdims=True)
    h2 = x1 * lax.rsqrt(ms + NORM_EPS) * n2_ref[...]
    h2 = h2 * (1.0 + mod_ref[0, 4:5, :]) + mod_ref[0, 3:4, :]
    h2_ref[0] = h2.astype(BF16)
    logits = lax.dot_general(rw_ref[...], h2, (((1,), (1,)), ((), ())),
                             precision=HIGHEST, preferred_element_type=F32)
    mx = jnp.max(logits, axis=0, keepdims=True)
    ex = jnp.exp(logits - mx)
    aff_ref[0] = ex / jnp.sum(ex, axis=0, keepdims=True)


def _merge(y_f, y_b, bonus, z_lora, at, gates, x, mod, lnx_w, lnx_b, g2, w_br_rwkv, w_br_attn, w_out,
           norm2_g, router_wt, tm):
    b, l, d = x.shape
    rdim = y_f.shape[-1]
    ne = router_wt.shape[0]
    const = lambda i, t: (0, 0)
    return pl.pallas_call(
        _merge_kernel,
        grid=(b, l // tm),
        in_specs=[
            pl.BlockSpec((1, tm, rdim), lambda i, t: (i, t, 0)),
            pl.BlockSpec((1, tm, rdim), lambda i, t: (i, t, 0)),
            pl.BlockSpec((1, tm, rdim), lambda i, t: (i, t, 0)),
            pl.BlockSpec((1, tm, LANES), lambda i, t: (i, t, 2)),
            pl.BlockSpec((1, tm, at.shape[-1]), lambda i, t: (i, t, 0)),
            pl.BlockSpec((1, tm, 2 * d), lambda i, t: (i, t, 0)),
            pl.BlockSpec((1, tm, d), lambda i, t: (i, t, 0)),
            pl.BlockSpec((1, 6, d), lambda i, t: (i, 0, 0)),
            pl.BlockSpec((1, rdim), const),
            pl.BlockSpec((1, rdim), const),
            pl.BlockSpec(g2.shape, const),
            pl.BlockSpec(w_br_rwkv.shape, const),
            pl.BlockSpec(w_br_attn.shape, const),
            pl.BlockSpec(w_out.shape, const),
            pl.BlockSpec((1, d), const),
            pl.BlockSpec(router_wt.shape, const),
        ],
        out_specs=[
            pl.BlockSpec((1, tm, d), lambda i, t: (i, t, 0)),
            pl.BlockSpec((1, tm, d), lambda i, t: (i, t, 0)),
            pl.BlockSpec((1, ne, tm), lambda i, t: (i, 0, t)),
        ],
        out_shape=[
            jax.ShapeDtypeStruct((b, l, d), F32),
            jax.ShapeDtypeStruct((b, l, d), BF16),
            jax.ShapeDtypeStruct((b, ne, l), F32),
        ],
        compiler_params=_params(("parallel", "parallel")),
        name="merge",
    )(y_f, y_b, bonus, z_lora, at, gates, x, mod, lnx_w.reshape(1, rdim), lnx_b.reshape(1, rdim), g2,
      w_br_rwkv, w_br_attn, w_out, norm2_g.reshape(1, d), router_wt)


def _prefix_count(mask_f, tri_excl):
    rows, l = mask_f.shape
    running = jnp.zeros((rows, 1), F32)
    pieces = []
    for blk in range(l // LANES):
        m = mask_f[:, blk * LANES:(blk + 1) * LANES]
        pieces.append(_dot(m.astype(BF16), tri_excl) + running)
        running = running + jnp.sum(m, axis=1, keepdims=True)
    return jnp.concatenate(pieces, axis=1)


def _route_kernel(aff_ref, sel_ref, *, cap):
    aff = aff_ref[0]
    bits = lax.bitcast_convert_type(aff, jnp.int32)
    thr = jnp.zeros((aff.shape[0], 1), jnp.int32)
    for bit in range(30, -1, -1):
        cand = thr | (1 << bit)
        cnt = jnp.sum(jnp.where(bits >= cand, 1.0, 0.0), axis=1, keepdims=True)
        thr = jnp.where(cnt >= cap, cand, thr)
    r = lax.broadcasted_iota(jnp.int32, (LANES, LANES), 0)
    c = lax.broadcasted_iota(jnp.int32, (LANES, LANES), 1)
    tri_excl = jnp.where(r < c, 1.0, 0.0).astype(BF16)
    gt = jnp.where(bits > thr, 1.0, 0.0)
    eq = jnp.where(bits == thr, 1.0, 0.0)
    need = cap - jnp.sum(gt, axis=1, keepdims=True)
    eq_rank = _prefix_count(eq, tri_excl)
    chosen = jnp.maximum(gt, jnp.where(eq_rank < need, eq, 0.0))
    rank = _prefix_count(chosen, tri_excl)
    sel_ref[0] = jnp.where(chosen > 0.5, rank, -1.0).astype(jnp.int32)


def _route(aff_t, cap):
    b, ne, l = aff_t.shape
    return pl.pallas_call(
        functools.partial(_route_kernel, cap=cap),
        grid=(b,),
        in_specs=[pl.BlockSpec((1, ne, l), lambda i: (i, 0, 0))],
        out_specs=pl.BlockSpec((1, ne, l), lambda i: (i, 0, 0)),
        out_shape=jax.ShapeDtypeStruct((b, ne, l), jnp.int32),
        compiler_params=_params(("parallel",)),
        name="route",
    )(aff_t)


def _moe_kernel(sel_ref, aff_ref, h_ref, x1_ref, mod_ref, wg_ref, wu_ref, wd_ref, o_ref, xe_ref, ye_ref, g_ref, *,
                cap, n_f, scatter_tile):
    e = pl.program_id(1)
    f = pl.program_id(2)
    l = h_ref.shape[1]

    @pl.when((e == 0) & (f == 0))
    def _():
        o_ref[...] = x1_ref[...]

    @pl.when(f == 0)
    def _():
        sel_row = sel_ref[0, pl.ds(e, 1), :]
        slot = lax.broadcasted_iota(jnp.int32, (cap, l), 0)
        onehot = jnp.where(slot == sel_row, 1.0, 0.0).astype(BF16)
        g_ref[...] = onehot
        xe_ref[...] = _dot(onehot, h_ref[0]).astype(BF16)
        ye_ref[...] = jnp.zeros(ye_ref.shape, F32)

    xe = xe_ref[...]
    hg = _dot(xe, wg_ref[0])
    hu = _dot(xe, wu_ref[0])
    act = (hg * _sigmoid(hg) * hu).astype(BF16)
    ye_ref[...] += _dot(act, wd_ref[0])

    @pl.when(f == n_f - 1)
    def _():
        sel_row = sel_ref[0, pl.ds(e, 1), :]
        aff_row = aff_ref[0, pl.ds(e, 1), :]
        slot = lax.broadcasted_iota(jnp.int32, (cap, l), 0)
        val = jnp.sum(jnp.where(slot == sel_row, aff_row, 0.0), axis=1, keepdims=True)
        yw = (ye_ref[...] * val * mod_ref[0, 5:6, :]).astype(BF16)
        for lt in range(l // scatter_tile):
            sl = slice(lt * scatter_tile, (lt + 1) * scatter_tile)
            o_ref[0, sl, :] += _dot_tn(g_ref[:, sl], yw)


def _moe(sel, aff_t, h2, x1, mod, wg, wu, wd, cap, n_f):
    b, l, d = h2.shape
    ne = sel.shape[1]
    ff = wg.shape[-1]
    fc = ff // n_f
    scatter_tile = min(l, 512)
    per_sample = lambda i, e, f: (i, 0, 0)
    return pl.pallas_call(
        functools.partial(_moe_kernel, cap=cap, n_f=n_f, scatter_tile=scatter_tile),
        grid=(b, ne, n_f),
        in_specs=[
            pl.BlockSpec((1, ne, l), per_sample),
            pl.BlockSpec((1, ne, l), per_sample),
            pl.BlockSpec((1, l, d), per_sample, pipeline_mode=pl.Buffered(1)),
            pl.BlockSpec((1, l, d), per_sample, pipeline_mode=pl.Buffered(1)),
            pl.BlockSpec((1, 6, d), per_sample),
            pl.BlockSpec((1, d, fc), lambda i, e, f: (e, 0, f)),
            pl.BlockSpec((1, d, fc), lambda i, e, f: (e, 0, f)),
            pl.BlockSpec((1, fc, d), lambda i, e, f: (e, f, 0)),
        ],
        out_specs=pl.BlockSpec((1, l, d), lambda i, e, f: (i, 0, 0)),
        out_shape=jax.ShapeDtypeStruct((b, l, d), F32),
        scratch_shapes=[
            pltpu.VMEM((cap, d), BF16),
            pltpu.VMEM((cap, d), F32),
            pltpu.VMEM((cap, l), BF16),
        ],
        compiler_params=_params(("parallel", "arbitrary", "arbitrary")),
        name="moe",
    )(sel, aff_t, h2, x1, mod, wg, wu, wd)


def _pad_rows(w, row0, rows):
    return jnp.zeros((rows, w.shape[1]), w.dtype).at[row0:row0 + w.shape[0]].set(w)


def _layer(x, ctx, mod_lat, mod_ctx, p):
    b, l, d = x.shape
    lc = ctx.shape[1]
    rdim = p["rwkv_w0"].shape[-1]
    dl = p["rwkv_w2"].shape[-2]
    al = p["rwkv_a2"].shape[-2]
    gl = p["rwkv_g2"].shape[0]
    att_dim = p["w_br_attn"].shape[0]
    kv_dim = (p["w_in"].shape[1] - 3 * rdim - 2 * dl - 2 * al - gl - att_dim - 2 * d) // 2
    assert 2 * dl == LANES and 2 * al == LANES and gl == LANES

    w_in = p["w_in"].astype(BF16)
    o = 0
    w_rkv = w_in[:, o:o + 3 * rdim]; o += 3 * rdim
    w_lora = w_in[:, o:o + 2 * dl + 2 * al + gl]; o += 2 * dl + 2 * al + gl
    w_q = w_in[:, o:o + att_dim]; o += att_dim
    w_kv = w_in[:, o:o + 2 * kv_dim]; o += 2 * kv_dim
    w_gates = w_in[:, o:o + 2 * d]

    tm = min(l, 512)
    z_rkv, z_lora, q, kv, gates = _proj(x, mod_lat, p["norm1_g"], [w_rkv, w_lora, w_q, w_kv, w_gates],
                                        min(l, 256), True)
    tmc = min(lc, 256)
    zc_rkv, zc_lora, kv_c = _proj(ctx, mod_ctx, p["norm1_g"], [w_rkv, w_lora, w_kv], tmc, False)

    conv_w8 = _pad_rows(p["rwkv_conv"], 0, 8)
    zeros3 = jnp.zeros((3, rdim), F32)
    pvec = jnp.stack([
        jnp.concatenate([p["rwkv_w0"][dd][None], p["rwkv_a0"][dd][None], p["rwkv_kk"][dd][None],
                         p["rwkv_ka"][dd][None], p["rwkv_rk"][dd].reshape(1, rdim), zeros3], axis=0)
        for dd in range(2)])
    w2_pad = jnp.stack([_pad_rows(p["rwkv_w2"][dd], dd * dl, LANES) for dd in range(2)]).astype(BF16)
    a2_pad = jnp.stack([_pad_rows(p["rwkv_a2"][dd], dd * al, LANES) for dd in range(2)]).astype(BF16)
    n_stack = rdim // MXU_DIM
    s0 = jnp.zeros((b, 2, n_stack, MXU_DIM, MXU_DIM), F32)
    (s_ctx,) = _wkv2(zc_rkv, zc_lora, conv_w8, pvec, w2_pad, a2_pad, s0, False)
    y_f, y_b, bonus, _ = _wkv2(z_rkv, z_lora, conv_w8, pvec, w2_pad, a2_pad, s_ctx, True)

    cos, sin = _rope_tables(l)
    reps = LANES // HEAD_DIM
    q_g = jnp.tile(p["q_norm_g"], reps).reshape(1, LANES)
    k_g = jnp.tile(p["k_norm_g"], reps).reshape(1, LANES)
    at = _attn(q, kv, kv_c, cos, sin, q_g, k_g, p["attn_sink"])

    x1, h2, aff_t = _merge(y_f, y_b, bonus, z_lora, at, gates, x, mod_lat, p["lnx_w"], p["lnx_b"],
                           p["rwkv_g2"].astype(BF16), p["w_br_rwkv"].astype(BF16),
                           p["w_br_attn"].astype(BF16), p["w_out"].astype(BF16), p["norm2_g"],
                           p["router_w"].T, tm)
    ne = p["router_w"].shape[1]
    cap = EC_FACTOR * l // ne
    sel = _route(aff_t, cap)
    n_f = 2 if p["exp_w_gate"].shape[-1] % (2 * MXU_DIM) == 0 else 1
    return _moe(sel, aff_t, h2, x1, mod_lat, p["exp_w_gate"].astype(BF16), p["exp_w_up"].astype(BF16),
                p["exp_w_down"].astype(BF16), cap, n_f)


def kernel(x, c, ctx, c_ctx, ada_w, ada_b, norm1_g, norm2_g, w_in, rwkv_conv, rwkv_w0, rwkv_w2, rwkv_a0, rwkv_a2, rwkv_kk, rwkv_ka, rwkv_rk, rwkv_g2, lnx_w, lnx_b, q_norm_g, k_norm_g, attn_sink, w_br_rwkv, w_br_attn, w_out, router_w, exp_w_gate, exp_w_up, exp_w_down):
    b, l, d = x.shape
    depth = ada_w.shape[0]
    assert depth == 1, "the context stream is only read, never advanced, for a single layer"
    rows = -(-(b + 1) // 8) * 8
    c_all = jnp.zeros((rows, d), F32).at[:b].set(c).at[b].set(c_ctx)
    names = ["ada_w", "ada_b", "norm1_g", "norm2_g", "w_in", "rwkv_conv", "rwkv_w0", "rwkv_w2", "rwkv_a0",
             "rwkv_a2", "rwkv_kk", "rwkv_ka", "rwkv_rk", "rwkv_g2", "lnx_w", "lnx_b", "q_norm_g", "k_norm_g",
             "attn_sink", "w_br_rwkv", "w_br_attn", "w_out", "router_w", "exp_w_gate", "exp_w_up", "exp_w_down"]
    vals = [ada_w, ada_b, norm1_g, norm2_g, w_in, rwkv_conv, rwkv_w0, rwkv_w2, rwkv_a0, rwkv_a2, rwkv_kk,
            rwkv_ka, rwkv_rk, rwkv_g2, lnx_w, lnx_b, q_norm_g, k_norm_g, attn_sink, w_br_rwkv, w_br_attn,
            w_out, router_w, exp_w_gate, exp_w_up, exp_w_down]
    p = {n: v[0] for n, v in zip(names, vals)}
    mod = _adaln(c_all, p["ada_w"], p["ada_b"])
    mod_lat = mod[:b].reshape(b, 6, d)
    mod_ctx = mod[b:b + 1].reshape(1, 6, d)
    return _layer(x, ctx, mod_lat, mod_ctx, p)
```

```python
import functools

import jax
import jax.numpy as jnp
from jax import lax
from jax.experimental import pallas as pl
from jax.experimental.pallas import tpu as pltpu

F32 = jnp.float32
BF16 = jnp.bfloat16
HIGHEST = lax.Precision.HIGHEST

HEAD_DIM = 64
HEAD_SHIFT = 6
GRID_W = 64
WINDOW = 128
ATT_BLOCK = 128
ROPE_BASE = 10000.0
NORM_EPS = 1e-6
LNX_EPS = 64e-5
MASK_VALUE = -1e30
EC_FACTOR = 2

LANES = 128
MXU_DIM = 256
VMEM_LIMIT_BYTES = 56 * 1024 * 1024

WKV_CHUNK = 64
WKV_SAMPLES_PER_STEP = 4
HEADS_PER_STACK = MXU_DIM // HEAD_DIM


def _dot(a, b):
    return jnp.dot(a, b, preferred_element_type=F32)


def _dot_nt(a, b):
    return lax.dot_general(a, b, (((1,), (1,)), ((), ())), preferred_element_type=F32)


def _dot_tn(a, b):
    return lax.dot_general(a, b, (((0,), (0,)), ((), ())), preferred_element_type=F32)


def _sigmoid(x):
    return 1.0 / (1.0 + jnp.exp(-x))


def _split_bf16(x, parts):
    out = []
    rem = x
    for _ in range(parts):
        hi = rem.astype(BF16)
        out.append(hi)
        rem = rem - hi.astype(F32)
    return out


def _dot_split_rhs(a_bf16, x, parts):
    acc = None
    for term in _split_bf16(x, parts):
        d = _dot(a_bf16, term)
        acc = d if acc is None else acc + d
    return acc


def _dot_split_lhs(x, b_bf16, parts):
    acc = None
    for term in _split_bf16(x, parts):
        d = _dot(term, b_bf16)
        acc = d if acc is None else acc + d
    return acc


def _head_ones(width):
    r = lax.broadcasted_iota(jnp.int32, (width, width), 0) >> HEAD_SHIFT
    c = lax.broadcasted_iota(jnp.int32, (width, width), 1) >> HEAD_SHIFT
    return jnp.where(r == c, 1.0, 0.0).astype(BF16)


def _head_sum(x, ones_bd):
    return _dot_split_lhs(x, ones_bd, 2)


def _params(semantics):
    return pltpu.CompilerParams(dimension_semantics=semantics, vmem_limit_bytes=VMEM_LIMIT_BYTES)


def _adaln_kernel(c_ref, w_ref, b_ref, o_ref):
    c = c_ref[...]
    s = c * _sigmoid(c)
    o_ref[...] = jnp.dot(s, w_ref[...], precision=HIGHEST, preferred_element_type=F32) + b_ref[...]


def _adaln(c_all, ada_w, ada_b):
    rows, d = c_all.shape
    n = ada_w.shape[1]
    tn = 512
    return pl.pallas_call(
        _adaln_kernel,
        grid=(n // tn,),
        in_specs=[
            pl.BlockSpec((rows, d), lambda j: (0, 0)),
            pl.BlockSpec((d, tn), lambda j: (0, j)),
            pl.BlockSpec((1, tn), lambda j: (0, j)),
        ],
        out_specs=pl.BlockSpec((rows, tn), lambda j: (0, j)),
        out_shape=jax.ShapeDtypeStruct((rows, n), F32),
        compiler_params=_params(("parallel",)),
        name="adaln",
    )(c_all, ada_w, ada_b.reshape(1, n))


def _proj_kernel(x_ref, mod_ref, g_ref, *refs):
    n_out = len(refs) // 2
    w_refs, o_refs = refs[:n_out], refs[n_out:]
    x = x_ref[0]
    ms = jnp.mean(x * x, axis=-1, keepdims=True)
    y = x * lax.rsqrt(ms + NORM_EPS) * g_ref[...]
    shift = mod_ref[0, 0:1, :]
    scale = mod_ref[0, 1:2, :]
    h = (y * (1.0 + scale) + shift).astype(BF16)
    for w_ref, o_ref in zip(w_refs, o_refs):
        o_ref[0] = _dot(h, w_ref[...])


def _proj(x, mod, norm_g, weights, tm, per_sample_mod):
    b, l, d = x.shape
    mod_map = (lambda i, t: (i, 0, 0)) if per_sample_mod else (lambda i, t: (0, 0, 0))
    in_specs = [
        pl.BlockSpec((1, tm, d), lambda i, t: (i, t, 0)),
        pl.BlockSpec((1, 6, d), mod_map),
        pl.BlockSpec((1, d), lambda i, t: (0, 0)),
    ]
    out_specs, out_shapes = [], []
    for w in weights:
        n = w.shape[1]
        in_specs.append(pl.BlockSpec((d, n), lambda i, t: (0, 0)))
        out_specs.append(pl.BlockSpec((1, tm, n), lambda i, t: (i, t, 0)))
        out_shapes.append(jax.ShapeDtypeStruct((b, l, n), F32))
    return pl.pallas_call(
        _proj_kernel,
        grid=(b, l // tm),
        in_specs=in_specs,
        out_specs=out_specs,
        out_shape=out_shapes,
        compiler_params=_params(("parallel", "parallel")),
        name="proj",
    )(x, mod, norm_g.reshape(1, d), *weights)


def _wkv_kernel(zc_ref, zp_ref, zn_ref, zl_ref, cw_ref, pv_ref, w2_ref, a2_ref, s0_ref,
                y_ref, bo_ref, s1_ref, s_ref, *, n_chunks):
    t = WKV_CHUNK
    rdim = y_ref.shape[-1]
    n_stack = rdim // MXU_DIM
    d = pl.program_id(1)
    j = pl.program_id(2)
    rev = d == 1
    order = 1 - 2 * d
    chunk = jnp.where(rev, n_chunks - 1 - j, j)

    @pl.when(j == 0)
    def _():
        s_ref[...] = s0_ref[0, 0]

    zm = zc_ref[0]
    row = lax.broadcasted_iota(jnp.int32, zm.shape, 0)
    prev_row = jnp.where(chunk > 0, zp_ref[0, 7:8, :], 0.0)
    next_row = jnp.where(chunk < n_chunks - 1, zn_ref[0, 0:1, :], 0.0)
    z_up = jnp.where(row == 0, prev_row, pltpu.roll(zm, 1, 0))
    z_dn = jnp.where(row == t - 1, next_row, pltpu.roll(zm, t - 1, 0))
    rkv = cw_ref[0:1, :] * z_up + cw_ref[1:2, :] * zm + cw_ref[2:3, :] * z_dn
    r = rkv[:, 0:rdim]
    k = rkv[:, rdim:2 * rdim]
    v = rkv[:, 2 * rdim:3 * rdim]

    w0 = pv_ref[0, 0:1, :]
    a0 = pv_ref[0, 1:2, :]
    kk_p = pv_ref[0, 2:3, :]
    ka_p = pv_ref[0, 3:4, :]
    rk_p = pv_ref[0, 4:5, :]

    wd = zl_ref[0, :, 0:LANES]
    ad = zl_ref[0, :, LANES:2 * LANES]
    wl = w0 + _dot(jnp.tanh(wd).astype(BF16), w2_ref[0])
    neg = -wl
    softplus = jnp.maximum(neg, 0.0) + jnp.log(1.0 + jnp.exp(-jnp.abs(neg)))
    lw = -jnp.exp(-softplus - 0.5)
    asig = _sigmoid(a0 + _dot(ad.astype(BF16), a2_ref[0]))

    ones_bd = _head_ones(MXU_DIM)
    kk = k * kk_p
    k_mod = k * (1.0 + (asig - 1.0) * ka_p)
    rkk = r * k_mod * rk_p

    tr = lax.broadcasted_iota(jnp.int32, (t, t), 0)
    tc = lax.broadcasted_iota(jnp.int32, (t, t), 1)
    tri = jnp.where((tr - tc) * order >= 0, 1.0, 0.0).astype(BF16)
    cum = _dot_split_rhs(tri, lw, 3)
    c_end = jnp.where(rev, cum[0:1, :], cum[t - 1:t, :])
    rho = 0.5 * c_end
    e_in = jnp.exp(cum - rho)
    e_out = jnp.exp(rho - cum)
    e_ex = jnp.exp(cum - lw - rho)
    e_rho = jnp.exp(rho)
    e_end = jnp.exp(c_end)

    sr = lax.broadcasted_iota(jnp.int32, (MXU_DIM, MXU_DIM), 0)
    sc = lax.broadcasted_iota(jnp.int32, (MXU_DIM, MXU_DIM), 1)
    block_mask = (sr >> HEAD_SHIFT) == (sc >> HEAD_SHIFT)
    ahead = (sr - sc) * order
    strict = ahead > 0
    incl = ahead >= 0
    eye = jnp.where(sr == sc, 1.0, 0.0)

    def stack(x):
        xs = jnp.concatenate([x] * HEADS_PER_STACK, axis=0)
        return jnp.where(block_mask, xs, 0.0).astype(BF16)

    for q in range(n_stack):
        sl = slice(q * MXU_DIM, (q + 1) * MXU_DIM)
        kk_q = kk[:, sl]
        ss = _head_sum(kk_q * kk_q, ones_bd)
        kkn = kk_q * lax.rsqrt(jnp.maximum(ss, 1e-24))
        a_vec = -kkn
        b_vec = kkn * asig[:, sl]
        bo_ref[0, 0, :, sl] = _head_sum(rkk[:, sl], ones_bd) * v[:, sl]

        er = e_rho[:, sl]
        a_t = a_vec * e_ex[:, sl]
        r_t = r[:, sl] * e_in[:, sl]
        b_t = b_vec * e_out[:, sl]
        k_t = k_mod[:, sl] * e_out[:, sl]
        a_st, r_st, b_st, k_st = stack(a_t), stack(r_t), stack(b_t), stack(k_t)
        a0_st, r0_st = stack(a_t * er), stack(r_t * er)
        bh_st, kh_st = stack(b_t * er), stack(k_t * er)
        v_st = stack(v[:, sl])

        n_mat = jnp.where(strict, _dot_nt(a_st, b_st), 0.0)
        a_ak = jnp.where(strict, _dot_nt(a_st, k_st), 0.0).astype(BF16)
        a_rb = jnp.where(incl, _dot_nt(r_st, b_st), 0.0).astype(BF16)
        a_rk = jnp.where(incl, _dot_nt(r_st, k_st), 0.0).astype(BF16)

        p = n_mat.astype(BF16)
        inv = eye + n_mat
        steps = t.bit_length() - 2
        for _ in range(steps):
            p32 = _dot(p, p)
            p = p32.astype(BF16)
            inv = inv + _dot(inv.astype(BF16), p)
        inv = inv.astype(BF16)

        s_q = s_ref[q]
        s_b = s_q.astype(BF16)
        x_mat = _dot_nt(a0_st, s_b) + _dot(a_ak, v_st)
        z_mat = _dot(inv, x_mat.astype(BF16))
        z_b = z_mat.astype(BF16)
        y_mat = _dot_nt(r0_st, s_b) + _dot(a_rb, z_b) + _dot(a_rk, v_st)
        s_ref[q] = s_q * e_end[:, sl] + _dot_tn(z_b, bh_st) + _dot_tn(v_st, kh_st)
        y_q = y_mat[0:t]
        for hh in range(1, HEADS_PER_STACK):
            y_q = y_q + y_mat[hh * t:(hh + 1) * t]
        y_ref[0, 0, :, sl] = y_q

    @pl.when(j == n_chunks - 1)
    def _():
        s1_ref[0, 0] = s_ref[...]


def _wkv(z_rkv, z_lora, conv_w8, pvec, w2_pad, a2_pad, s0):
    b, l, c3 = z_rkv.shape
    rdim = c3 // 3
    t = WKV_CHUNK
    nc = l // t
    n_stack = rdim // MXU_DIM
    hb = t // 8

    def cidx(dd, jj):
        return jnp.where(dd == 1, nc - 1 - jj, jj)

    in_specs = [
        pl.BlockSpec((1, t, c3), lambda i, dd, jj: (i, cidx(dd, jj), 0)),
        pl.BlockSpec((1, 8, c3), lambda i, dd, jj: (i, jnp.maximum(cidx(dd, jj) * hb - 1, 0), 0)),
        pl.BlockSpec((1, 8, c3), lambda i, dd, jj: (i, jnp.minimum((cidx(dd, jj) + 1) * hb, l // 8 - 1), 0)),
        pl.BlockSpec((1, t, 2 * LANES), lambda i, dd, jj: (i, cidx(dd, jj), 0)),
        pl.BlockSpec((8, c3), lambda i, dd, jj: (0, 0)),
        pl.BlockSpec((1, 8, rdim), lambda i, dd, jj: (dd, 0, 0)),
        pl.BlockSpec((1, LANES, rdim), lambda i, dd, jj: (dd, 0, 0)),
        pl.BlockSpec((1, LANES, rdim), lambda i, dd, jj: (dd, 0, 0)),
        pl.BlockSpec((1, 1, n_stack, MXU_DIM, MXU_DIM), lambda i, dd, jj: (i, dd, 0, 0, 0)),
    ]
    out_specs = [
        pl.BlockSpec((1, 1, t, rdim), lambda i, dd, jj: (i, dd, cidx(dd, jj), 0)),
        pl.BlockSpec((1, 1, t, rdim), lambda i, dd, jj: (i, dd, cidx(dd, jj), 0)),
        pl.BlockSpec((1, 1, n_stack, MXU_DIM, MXU_DIM), lambda i, dd, jj: (i, dd, 0, 0, 0)),
    ]
    out_shapes = [
        jax.ShapeDtypeStruct((b, 2, l, rdim), F32),
        jax.ShapeDtypeStruct((b, 2, l, rdim), F32),
        jax.ShapeDtypeStruct((b, 2, n_stack, MXU_DIM, MXU_DIM), F32),
    ]
    return pl.pallas_call(
        functools.partial(_wkv_kernel, n_chunks=nc),
        grid=(b, 2, nc),
        in_specs=in_specs,
        out_specs=out_specs,
        out_shape=out_shapes,
        scratch_shapes=[pltpu.VMEM((n_stack, MXU_DIM, MXU_DIM), F32)],
        compiler_params=_params(("parallel", "parallel", "arbitrary")),
        name="wkv",
    )(z_rkv, z_rkv, z_rkv, z_lora, conv_w8, pvec, w2_pad, a2_pad, s0)


def _wkv_chunk_inputs(bi, dd, chunk, n_chunks, zc_ref, zp_ref, zn_ref, zl_ref, cw_ref, pv_ref, w2_ref, a2_ref):
    t = WKV_CHUNK
    rdim = pv_ref.shape[-1]
    zm = zc_ref[bi]
    row = lax.broadcasted_iota(jnp.int32, zm.shape, 0)
    prev_row = jnp.where(chunk > 0, zp_ref[bi, 7:8, :], 0.0)
    next_row = jnp.where(chunk < n_chunks - 1, zn_ref[bi, 0:1, :], 0.0)
    z_up = jnp.where(row == 0, prev_row, pltpu.roll(zm, 1, 0))
    z_dn = jnp.where(row == t - 1, next_row, pltpu.roll(zm, t - 1, 0))
    rkv = cw_ref[0:1, :] * z_up + cw_ref[1:2, :] * zm + cw_ref[2:3, :] * z_dn
    r = rkv[:, 0:rdim]
    k = rkv[:, rdim:2 * rdim]
    v = rkv[:, 2 * rdim:3 * rdim]
    wd = zl_ref[bi, :, 0:LANES]
    ad = zl_ref[bi, :, LANES:2 * LANES].astype(BF16)
    wl = pv_ref[dd, 0:1, :] + _dot(jnp.tanh(wd).astype(BF16), w2_ref[dd])
    neg = -wl
    softplus = jnp.maximum(neg, 0.0) + jnp.log(1.0 + jnp.exp(-jnp.abs(neg)))
    lw = -jnp.exp(-softplus - 0.5)
    asig = _sigmoid(pv_ref[dd, 1:2, :] + _dot(ad, a2_ref[dd]))
    return r, k, v, lw, asig, ad


def _wkv2_kernel(*refs, n_chunks, emit_y):
    t = WKV_CHUNK
    (zcf, zpf, znf, zlf, zcb, zpb, znb, zlb, cw_ref, pv_ref, w2_ref, a2_ref, s0_ref) = refs[:13]
    if emit_y:
        yf_ref, yb_ref, bo_ref, s1_ref, s_ref = refs[13:]
    else:
        s1_ref, s_ref = refs[13:]
    rdim = pv_ref.shape[-1]
    n_stack = rdim // MXU_DIM
    n_samples = zcf.shape[0]
    j = pl.program_id(1)

    @pl.when(j == 0)
    def _():
        s_ref[...] = s0_ref[...]

    ones_bd = _head_ones(MXU_DIM)
    sr = lax.broadcasted_iota(jnp.int32, (MXU_DIM, MXU_DIM), 0)
    sc = lax.broadcasted_iota(jnp.int32, (MXU_DIM, MXU_DIM), 1)
    stack_mask = jnp.where((sr >> HEAD_SHIFT) == (sc >> HEAD_SHIFT), 1.0, 0.0).astype(BF16)
    eye_b = jnp.where(sr == sc, 1.0, 0.0).astype(BF16)
    tr = lax.broadcasted_iota(jnp.int32, (t, t), 0)
    tc = lax.broadcasted_iota(jnp.int32, (t, t), 1)

    def stack(x):
        xb = x.astype(BF16)
        return jnp.concatenate([xb] * HEADS_PER_STACK, axis=0) * stack_mask

    chains = []
    for bi, dd in [(bi, dd) for bi in range(n_samples) for dd in range(2)]:
        rev = dd == 1
        chunk = (n_chunks - 1 - j) if rev else j
        io = (zcb, zpb, znb, zlb) if rev else (zcf, zpf, znf, zlf)
        r, k, v, lw, asig, ad = _wkv_chunk_inputs(bi, dd, chunk, n_chunks, *io, cw_ref, pv_ref, w2_ref, a2_ref)
        kk = k * pv_ref[dd, 2:3, :]
        k_mod = k * (1.0 + (asig - 1.0) * pv_ref[dd, 3:4, :])
        if emit_y and not rev:
            asig_o = _sigmoid(pv_ref[1, 1:2, :] + _dot(ad, a2_ref[1]))
            k_mod_o = k * (1.0 + (asig_o - 1.0) * pv_ref[1, 3:4, :])
            rkk = r * (k_mod * pv_ref[0, 4:5, :] + k_mod_o * pv_ref[1, 4:5, :])

        tri = jnp.where((tr <= tc) if rev else (tr >= tc), 1.0, 0.0).astype(BF16)
        cum = _dot_split_rhs(tri, lw, 3)
        c_end = cum[0:1, :] if rev else cum[t - 1:t, :]
        rho = 0.5 * c_end
        e_in = jnp.exp(cum - rho)
        e_out = jnp.exp(rho - cum)
        e_ex = jnp.exp(cum - lw - rho)
        e_rho = jnp.exp(rho)
        e_end = jnp.exp(c_end)
        strict = (sr < sc) if rev else (sr > sc)
        incl = (sr <= sc) if rev else (sr >= sc)

        for q in range(n_stack):
            sl = slice(q * MXU_DIM, (q + 1) * MXU_DIM)
            kk_q = kk[:, sl]
            ss = _head_sum(kk_q * kk_q, ones_bd)
            kkn = kk_q * lax.rsqrt(jnp.maximum(ss, 1e-24))
            ch = dict(bi=bi, dd=dd, q=q, sl=sl, rev=rev, e_rho=e_rho[:, sl], e_end=e_end[:, sl])
            ch["a_st"] = stack(-kkn * e_ex[:, sl])
            ch["b_st"] = stack(kkn * asig[:, sl] * e_out[:, sl])
            ch["k_st"] = stack(k_mod[:, sl] * e_out[:, sl])
            ch["v_st"] = stack(v[:, sl])
            if emit_y:
                ch["ar_st"] = jnp.concatenate([ch["a_st"], stack(r[:, sl] * e_in[:, sl])], axis=0)
                nb = _dot_nt(ch["ar_st"], ch["b_st"])
                nk = _dot_nt(ch["ar_st"], ch["k_st"])
                ch["a_rb"] = jnp.where(incl, nb[MXU_DIM:], 0.0).astype(BF16)
                ch["akrk"] = jnp.concatenate([jnp.where(strict, nk[:MXU_DIM], 0.0).astype(BF16),
                                              jnp.where(incl, nk[MXU_DIM:], 0.0).astype(BF16)], axis=0)
                if not rev:
                    bo_ref[bi, :, sl] = _head_sum(rkk[:, sl], ones_bd) * v[:, sl]
            else:
                ch["ar_st"] = ch["a_st"]
                nb = _dot_nt(ch["a_st"], ch["b_st"])
                ch["akrk"] = jnp.where(strict, _dot_nt(ch["a_st"], ch["k_st"]), 0.0).astype(BF16)
            ch["p"] = jnp.where(strict, nb[:MXU_DIM], 0.0).astype(BF16)
            ch["inv"] = ch["p"] + eye_b
            chains.append(ch)

    for _ in range(t.bit_length() - 2):
        for ch in chains:
            ch["p"] = _dot(ch["p"], ch["p"]).astype(BF16)
        for ch in chains:
            ch["inv"] = _dot(ch["inv"], ch["p"] + eye_b).astype(BF16)
    for ch in chains:
        ch["s_q"] = s_ref[ch["bi"], ch["dd"], ch["q"]]
        ch["s_rho"] = (ch["s_q"] * ch["e_rho"]).astype(BF16)
        ch["xs"] = _dot_nt(ch["ar_st"], ch["s_rho"])
        ch["xv"] = _dot(ch["akrk"], ch["v_st"])
        ch["x"] = (ch["xs"][:MXU_DIM] + ch["xv"][:MXU_DIM]).astype(BF16)
    for ch in chains:
        ch["z"] = _dot(ch["inv"], ch["x"]).astype(BF16)
    for ch in chains:
        upd = _dot_tn(ch["z"], ch["b_st"]) + _dot_tn(ch["v_st"], ch["k_st"])
        s_ref[ch["bi"], ch["dd"], ch["q"]] = ch["s_q"] * ch["e_end"] + upd * ch["e_rho"]
    if emit_y:
        for ch in chains:
            y_mat = ch["xs"][MXU_DIM:] + ch["xv"][MXU_DIM:] + _dot(ch["a_rb"], ch["z"])
            y_q = y_mat[0:t]
            for hh in range(1, HEADS_PER_STACK):
                y_q = y_q + y_mat[hh * t:(hh + 1) * t]
            (yb_ref if ch["rev"] else yf_ref)[ch["bi"], :, ch["sl"]] = y_q

    @pl.when(j == n_chunks - 1)
    def _():
        s1_ref[...] = s_ref[...]


def _wkv4_kernel(*refs, n_chunks, emit_y):
    t = WKV_CHUNK
    (zcf, zpf, znf, zlf, zcb, zpb, znb, zlb, cw_ref, pv_ref, w2_ref, a2_ref, s0_ref) = refs[:13]
    if emit_y:
        yf_ref, yb_ref, bo_ref, s1_ref, s_ref = refs[13:]
    else:
        s1_ref, s_ref = refs[13:]
    rdim = pv_ref.shape[-1]
    n_stack = rdim // MXU_DIM
    n_samples = zcf.shape[0]
    j = pl.program_id(1)

    @pl.when(j == 0)
    def _():
        s_ref[...] = s0_ref[...]

    ones_bd = _head_ones(MXU_DIM)
    sr = lax.broadcasted_iota(jnp.int32, (MXU_DIM, MXU_DIM), 0)
    sc = lax.broadcasted_iota(jnp.int32, (MXU_DIM, MXU_DIM), 1)
    stack_mask = jnp.where((sr >> HEAD_SHIFT) == (sc >> HEAD_SHIFT), 1.0, 0.0).astype(BF16)
    eye_b = jnp.where(sr == sc, 1.0, 0.0).astype(BF16)
    tr = lax.broadcasted_iota(jnp.int32, (t, t), 0)
    tc = lax.broadcasted_iota(jnp.int32, (t, t), 1)
    row = lax.broadcasted_iota(jnp.int32, (t, MXU_DIM), 0)
    head_mask = jnp.where((sr >> HEAD_SHIFT) == (sc >> HEAD_SHIFT), 1.0, 0.0)

    def stack(x):
        xb = x.astype(BF16)
        return jnp.concatenate([xb] * HEADS_PER_STACK, axis=0) * stack_mask

    def prepare(bi, chains):
        for dd in range(2):
            rev = dd == 1
            chunk = (n_chunks - 1 - j) if rev else j
            zc, zp, zn, zl = (zcb, zpb, znb, zlb) if rev else (zcf, zpf, znf, zlf)
            has_prev = chunk > 0
            has_next = chunk < n_chunks - 1

            def conv(c0):
                cs = slice(c0, c0 + MXU_DIM)
                zm = zc[bi, :, cs]
                prev_row = jnp.where(has_prev, zp[bi, 7:8, cs], 0.0)
                next_row = jnp.where(has_next, zn[bi, 0:1, cs], 0.0)
                z_up = jnp.where(row == 0, prev_row, pltpu.roll(zm, 1, 0))
                z_dn = jnp.where(row == t - 1, next_row, pltpu.roll(zm, t - 1, 0))
                return cw_ref[0:1, cs] * z_up + cw_ref[1:2, cs] * zm + cw_ref[2:3, cs] * z_dn

            wd = zl[bi, :, 0:LANES]
            ad = zl[bi, :, LANES:2 * LANES].astype(BF16)
            wl = pv_ref[dd, 0:1, :] + _dot(jnp.tanh(wd).astype(BF16), w2_ref[dd])
            yield
            neg = -wl
            softplus = jnp.maximum(neg, 0.0) + jnp.log(1.0 + jnp.exp(-jnp.abs(neg)))
            lw = -jnp.exp(-softplus - 0.5)
            yield
            asig = _sigmoid(pv_ref[dd, 1:2, :] + _dot(ad, a2_ref[dd]))
            yield
            if emit_y and not rev:
                asig_o = _sigmoid(pv_ref[1, 1:2, :] + _dot(ad, a2_ref[1]))
                yield
            tri = jnp.where((tr <= tc) if rev else (tr >= tc), 1.0, 0.0).astype(BF16)
            cum = _dot_split_rhs(tri, lw, 3)
            yield
            c_end = cum[0:1, :] if rev else cum[t - 1:t, :]
            rho = 0.5 * c_end
            e_in = jnp.exp(cum - rho)
            yield
            e_out = jnp.exp(rho - cum)
            yield
            e_ex = jnp.exp(cum - lw - rho)
            e_rho = jnp.exp(rho)
            e_end = jnp.exp(c_end)
            yield
            strict = (sr < sc) if rev else (sr > sc)
            incl = (sr <= sc) if rev else (sr >= sc)

            for q in range(n_stack):
                sl = slice(q * MXU_DIM, (q + 1) * MXU_DIM)
                ch = dict(bi=bi, dd=dd, q=q, sl=sl, rev=rev, e_rho=e_rho[:, sl], e_end=e_end[:, sl])
                k_q = conv(rdim + q * MXU_DIM)
                yield
                v_q = conv(2 * rdim + q * MXU_DIM)
                yield
                kk_q = k_q * pv_ref[dd, 2:3, sl]
                ss = _head_sum(kk_q * kk_q, ones_bd)
                kkn = kk_q * lax.rsqrt(jnp.maximum(ss, 1e-24))
                yield
                k_mod = k_q * (1.0 + (asig[:, sl] - 1.0) * pv_ref[dd, 3:4, sl])
                a_u = (-kkn * e_ex[:, sl]).astype(BF16)
                b_u = (kkn * asig[:, sl] * e_out[:, sl]).astype(BF16)
                k_u = (k_mod * e_out[:, sl]).astype(BF16)
                v_u = v_q.astype(BF16)
                ch["v_u"] = v_u
                ch["bk_u"] = jnp.concatenate([b_u, k_u], axis=0)
                ch["a_st"] = stack(a_u)
                yield
                ch["b_st"] = stack(b_u)
                yield
                ch["k_st"] = stack(k_u)
                yield
                ch["v_st"] = stack(v_u)
                yield
                if emit_y:
                    r_q = conv(q * MXU_DIM)
                    yield
                    r_u = (r_q * e_in[:, sl]).astype(BF16)
                    ch["ar_u"] = jnp.concatenate([a_u, r_u], axis=0)
                    ch["ar_st"] = jnp.concatenate([ch["a_st"], stack(r_u)], axis=0)
                    yield
                    nb = _dot_nt(ch["ar_st"], ch["b_st"])
                    yield
                    nk = _dot_nt(ch["ar_st"], ch["k_st"])
                    yield
                    ch["a_rb"] = jnp.where(incl, nb[MXU_DIM:], 0.0).astype(BF16)
                    yield
                    ak = jnp.where(strict, nk[:MXU_DIM], 0.0).astype(BF16)
                    yield
                    ch["akrk"] = jnp.concatenate([ak, jnp.where(incl, nk[MXU_DIM:], 0.0).astype(BF16)], axis=0)
                    yield
                    if not rev:
                        k_mod_o = k_q * (1.0 + (asig_o[:, sl] - 1.0) * pv_ref[1, 3:4, sl])
                        rkk = r_q * (k_mod * pv_ref[0, 4:5, sl] + k_mod_o * pv_ref[1, 4:5, sl])
                        bo_ref[bi, :, sl] = _head_sum(rkk, ones_bd) * v_q
                        yield
                else:
                    ch["ar_u"] = a_u
                    nb = _dot_nt(ch["a_st"], ch["b_st"])
                    yield
                    ch["akrk"] = jnp.where(strict, _dot_nt(ch["a_st"], ch["k_st"]), 0.0).astype(BF16)
                    yield
                ch["p"] = jnp.where(strict, nb[:MXU_DIM], 0.0).astype(BF16)
                ch["inv"] = ch["p"] + eye_b
                chains.append(ch)
                yield

    def advance(chains, filler):
        def tick():
            if filler is not None:
                next(filler, None)

        for _ in range(t.bit_length() - 2):
            for ch in chains:
                ch["p"] = _dot(ch["p"], ch["p"]).astype(BF16)
                tick()
            for ch in chains:
                ch["inv"] = _dot(ch["inv"], ch["p"] + eye_b).astype(BF16)
                tick()
        def unstack(m):
            out = m[0:t]
            for hh in range(1, HEADS_PER_STACK):
                out = out + m[hh * t:(hh + 1) * t]
            return out

        for ch in chains:
            ch["s_q"] = s_ref[ch["bi"], ch["dd"], ch["q"]]
            ch["s_rho"] = (ch["s_q"] * ch["e_rho"]).astype(BF16)
            ch["xs"] = _dot_nt(ch["ar_u"], ch["s_rho"])
            tick()
            ch["xv"] = _dot(ch["akrk"], ch["v_st"])
            tick()
            xs_st = jnp.concatenate([ch["xs"][0:t]] * HEADS_PER_STACK, axis=0) * head_mask
            ch["x"] = (xs_st + ch["xv"][:MXU_DIM]).astype(BF16)
        for ch in chains:
            z32 = _dot(ch["inv"], ch["x"])
            tick()
            ch["z"] = z32.astype(BF16)
            ch["zv_u"] = jnp.concatenate([unstack(z32).astype(BF16), ch["v_u"]], axis=0)
        for ch in chains:
            upd = _dot_tn(ch["zv_u"], ch["bk_u"]) * head_mask
            tick()
            s_ref[ch["bi"], ch["dd"], ch["q"]] = ch["s_q"] * ch["e_end"] + upd * ch["e_rho"]
        if emit_y:
            for ch in chains:
                y_mat = ch["xv"][MXU_DIM:] + _dot(ch["a_rb"], ch["z"])
                tick()
                (yb_ref if ch["rev"] else yf_ref)[ch["bi"], :, ch["sl"]] = ch["xs"][t:2 * t] + unstack(y_mat)

    chain_lists = [[] for _ in range(n_samples)]
    preps = [prepare(bi, chain_lists[bi]) for bi in range(n_samples)]
    for _ in preps[0]:
        pass
    for bi in range(n_samples):
        filler = preps[bi + 1] if bi + 1 < n_samples else None
        advance(chain_lists[bi], filler)
        if filler is not None:
            for _ in filler:
                pass

    @pl.when(j == n_chunks - 1)
    def _():
        s1_ref[...] = s_ref[...]


def _wkv2(z_rkv, z_lora, conv_w8, pvec, w2_pad, a2_pad, s0, emit_y):
    b, l, c3 = z_rkv.shape
    rdim = c3 // 3
    t = WKV_CHUNK
    nc = l // t
    n_stack = rdim // MXU_DIM
    hb = t // 8
    last8 = l // 8 - 1
    bs = WKV_SAMPLES_PER_STEP if b % WKV_SAMPLES_PER_STEP == 0 else 1

    def chunk_specs(cidx):
        return [
            pl.BlockSpec((bs, t, c3), lambda i, jj: (i, cidx(jj), 0)),
            pl.BlockSpec((bs, 8, c3), lambda i, jj: (i, jnp.maximum(cidx(jj) * hb - 1, 0), 0)),
            pl.BlockSpec((bs, 8, c3), lambda i, jj: (i, jnp.minimum((cidx(jj) + 1) * hb, last8), 0)),
            pl.BlockSpec((bs, t, 2 * LANES), lambda i, jj: (i, cidx(jj), 0)),
        ]

    fwd = lambda jj: jj
    bwd = lambda jj: nc - 1 - jj
    state_spec = pl.BlockSpec((bs, 2, n_stack, MXU_DIM, MXU_DIM), lambda i, jj: (i, 0, 0, 0, 0))
    in_specs = chunk_specs(fwd) + chunk_specs(bwd) + [
        pl.BlockSpec((8, c3), lambda i, jj: (0, 0)),
        pl.BlockSpec((2, 8, rdim), lambda i, jj: (0, 0, 0)),
        pl.BlockSpec((2, LANES, rdim), lambda i, jj: (0, 0, 0)),
        pl.BlockSpec((2, LANES, rdim), lambda i, jj: (0, 0, 0)),
        state_spec,
    ]
    out_specs, out_shapes = [], []
    if emit_y:
        out_specs += [
            pl.BlockSpec((bs, t, rdim), lambda i, jj: (i, fwd(jj), 0)),
            pl.BlockSpec((bs, t, rdim), lambda i, jj: (i, bwd(jj), 0)),
            pl.BlockSpec((bs, t, rdim), lambda i, jj: (i, fwd(jj), 0)),
        ]
        out_shapes += [jax.ShapeDtypeStruct((b, l, rdim), F32)] * 3
    out_specs.append(state_spec)
    out_shapes.append(jax.ShapeDtypeStruct((b, 2, n_stack, MXU_DIM, MXU_DIM), F32))
    return pl.pallas_call(
        functools.partial(_wkv4_kernel, n_chunks=nc, emit_y=emit_y),
        grid=(b // bs, nc),
        in_specs=in_specs,
        out_specs=out_specs,
        out_shape=out_shapes,
        scratch_shapes=[pltpu.VMEM((bs, 2, n_stack, MXU_DIM, MXU_DIM), F32)],
        compiler_params=_params(("parallel", "arbitrary")),
        name="wkv",
    )(z_rkv, z_rkv, z_rkv, z_lora, z_rkv, z_rkv, z_rkv, z_lora, conv_w8, pvec, w2_pad, a2_pad, s0)


_M_A, _M_B, _M_K, _M_V, _M_AK, _M_P, _M_R, _M_RB, _M_RK = range(9)


def _wkv3_kernel(*refs, n_chunks, emit_y):
    t = WKV_CHUNK
    (zcf, zpf, znf, zlf, zcb, zpb, znb, zlb, cw_ref, pv_ref, w2_ref, a2_ref, s0_ref) = refs[:13]
    if emit_y:
        yf_ref, yb_ref, bo_ref, s1_ref, s_ref, mats_ref, inv_ref, rows_ref = refs[13:]
    else:
        s1_ref, s_ref, mats_ref, inv_ref, rows_ref = refs[13:]
    rdim = pv_ref.shape[-1]
    n_stack = rdim // MXU_DIM
    n_chain = 2 * n_stack
    s = pl.program_id(1)
    valid = s >= 1

    @pl.when(s == 0)
    def _():
        s_ref[...] = s0_ref[0]
        mats_ref[...] = jnp.zeros(mats_ref.shape, BF16)
        inv_ref[...] = jnp.zeros(inv_ref.shape, F32)
        rows_ref[...] = jnp.zeros(rows_ref.shape, F32)

    ones_bd = _head_ones(MXU_DIM)
    sr = lax.broadcasted_iota(jnp.int32, (MXU_DIM, MXU_DIM), 0)
    sc = lax.broadcasted_iota(jnp.int32, (MXU_DIM, MXU_DIM), 1)
    stack_mask = jnp.where((sr >> HEAD_SHIFT) == (sc >> HEAD_SHIFT), 1.0, 0.0).astype(BF16)
    eye = jnp.where(sr == sc, 1.0, 0.0)
    tr = lax.broadcasted_iota(jnp.int32, (t, t), 0)
    tc = lax.broadcasted_iota(jnp.int32, (t, t), 1)

    def stack(x):
        xb = x.astype(BF16)
        return jnp.concatenate([xb] * HEADS_PER_STACK, axis=0) * stack_mask

    def prep_inputs(dd):
        rev = dd == 1
        chunk = jnp.maximum(n_chunks - 1 - s, 0) if rev else jnp.minimum(s, n_chunks - 1)
        io = (zcb, zpb, znb, zlb) if rev else (zcf, zpf, znf, zlf)
        r, k, v, lw, asig, ad = _wkv_chunk_inputs(dd, chunk, n_chunks, *io, cw_ref, pv_ref, w2_ref, a2_ref)
        pi = dict(dd=dd, rev=rev, r=r, k=k, v=v, asig=asig, ad=ad)
        pi["kk"] = k * pv_ref[dd, 2:3, :]
        pi["k_mod"] = k * (1.0 + (asig - 1.0) * pv_ref[dd, 3:4, :])
        tri = jnp.where((tr <= tc) if rev else (tr >= tc), 1.0, 0.0).astype(BF16)
        cum = _dot_split_rhs(tri, lw, 3)
        c_end = cum[0:1, :] if rev else cum[t - 1:t, :]
        rho = 0.5 * c_end
        pi["e_in"] = jnp.exp(cum - rho)
        pi["e_out"] = jnp.exp(rho - cum)
        pi["e_ex"] = jnp.exp(cum - lw - rho)
        pi["e_rho"] = jnp.exp(rho)
        pi["e_end"] = jnp.exp(c_end)
        return pi

    def prep_chains(pi):
        rev = pi["rev"]
        strict = (sr < sc) if rev else (sr > sc)
        incl = (sr <= sc) if rev else (sr >= sc)
        out = []
        for q in range(n_stack):
            sl = slice(q * MXU_DIM, (q + 1) * MXU_DIM)
            kk_q = pi["kk"][:, sl]
            ss = _head_sum(kk_q * kk_q, ones_bd)
            kkn = kk_q * lax.rsqrt(jnp.maximum(ss, 1e-24))
            m = {}
            m[_M_A] = stack(-kkn * pi["e_ex"][:, sl])
            m[_M_B] = stack(kkn * pi["asig"][:, sl] * pi["e_out"][:, sl])
            m[_M_K] = stack(pi["k_mod"][:, sl] * pi["e_out"][:, sl])
            m[_M_V] = stack(pi["v"][:, sl])
            n_mat = jnp.where(strict, _dot_nt(m[_M_A], m[_M_B]), 0.0)
            m[_M_AK] = jnp.where(strict, _dot_nt(m[_M_A], m[_M_K]), 0.0).astype(BF16)
            m[_M_P] = n_mat.astype(BF16)
            if emit_y:
                m[_M_R] = stack(pi["r"][:, sl] * pi["e_in"][:, sl])
                m[_M_RB] = jnp.where(incl, _dot_nt(m[_M_R], m[_M_B]), 0.0).astype(BF16)
                m[_M_RK] = jnp.where(incl, _dot_nt(m[_M_R], m[_M_K]), 0.0).astype(BF16)
            out.append(dict(c=pi["dd"] * n_stack + q, mats=m, inv=eye + n_mat,
                            e_rho=pi["e_rho"][:, sl], e_end=pi["e_end"][:, sl]))
        return out

    def bonus(pi):
        asig_o = _sigmoid(pv_ref[1, 1:2, :] + _dot(pi["ad"], a2_ref[1]))
        k_mod_o = pi["k"] * (1.0 + (asig_o - 1.0) * pv_ref[1, 3:4, :])
        rkk = pi["r"] * (pi["k_mod"] * pv_ref[0, 4:5, :] + k_mod_o * pv_ref[1, 4:5, :])
        for q in range(n_stack):
            sl = slice(q * MXU_DIM, (q + 1) * MXU_DIM)
            bo_ref[0, :, sl] = _head_sum(rkk[:, sl], ones_bd) * pi["v"][:, sl]

    p_cur = [mats_ref[c, _M_P] for c in range(n_chain)]
    inv_cur = [inv_ref[c] for c in range(n_chain)]

    def doubling_step():
        for c in range(n_chain):
            p_cur[c] = _dot(p_cur[c], p_cur[c]).astype(BF16)
        for c in range(n_chain):
            inv_cur[c] = inv_cur[c] + _dot(inv_cur[c].astype(BF16), p_cur[c])

    n_double = t.bit_length() - 2
    assert n_double == 5
    pi0 = prep_inputs(0)
    doubling_step()
    new_chains = prep_chains(pi0)
    doubling_step()
    pi1 = prep_inputs(1)
    doubling_step()
    new_chains += prep_chains(pi1)
    doubling_step()
    if emit_y:
        bonus(pi0)
    doubling_step()

    s_old, s_rho, x_b, z_b = [], [], [], []
    for c in range(n_chain):
        s_old.append(s_ref[c // n_stack, c % n_stack])
        s_rho.append((s_old[c] * rows_ref[c, 0:1, :]).astype(BF16))
        x_b.append((_dot_nt(mats_ref[c, _M_A], s_rho[c]) + _dot(mats_ref[c, _M_AK], mats_ref[c, _M_V])).astype(BF16))
    for c in range(n_chain):
        z_b.append(_dot(inv_cur[c].astype(BF16), x_b[c]).astype(BF16))
    for c in range(n_chain):
        upd = _dot_tn(z_b[c], mats_ref[c, _M_B]) + _dot_tn(mats_ref[c, _M_V], mats_ref[c, _M_K])
        s_new = s_old[c] * rows_ref[c, 1:2, :] + upd * rows_ref[c, 0:1, :]
        s_ref[c // n_stack, c % n_stack] = jnp.where(valid, s_new, s_old[c])
    if emit_y:
        for c in range(n_chain):
            y_mat = (_dot_nt(mats_ref[c, _M_R], s_rho[c]) + _dot(mats_ref[c, _M_RB], z_b[c])
                     + _dot(mats_ref[c, _M_RK], mats_ref[c, _M_V]))
            y_q = y_mat[0:t]
            for hh in range(1, HEADS_PER_STACK):
                y_q = y_q + y_mat[hh * t:(hh + 1) * t]
            sl = slice((c % n_stack) * MXU_DIM, (c % n_stack + 1) * MXU_DIM)
            (yb_ref if c >= n_stack else yf_ref)[0, :, sl] = y_q

    for ch in new_chains:
        c = ch["c"]
        for slot, val in ch["mats"].items():
            mats_ref[c, slot] = val
        inv_ref[c] = ch["inv"]
        rows_ref[c, 0:1, :] = ch["e_rho"]
        rows_ref[c, 1:2, :] = ch["e_end"]

    @pl.when(s == n_chunks)
    def _():
        s1_ref[0] = s_ref[...]


def _wkv3(z_rkv, z_lora, conv_w8, pvec, w2_pad, a2_pad, s0, emit_y):
    b, l, c3 = z_rkv.shape
    rdim = c3 // 3
    t = WKV_CHUNK
    nc = l // t
    n_stack = rdim // MXU_DIM
    hb = t // 8
    last8 = l // 8 - 1

    def chunk_specs(cidx):
        return [
            pl.BlockSpec((1, t, c3), lambda i, s: (i, cidx(s), 0)),
            pl.BlockSpec((1, 8, c3), lambda i, s: (i, jnp.maximum(cidx(s) * hb - 1, 0), 0)),
            pl.BlockSpec((1, 8, c3), lambda i, s: (i, jnp.minimum((cidx(s) + 1) * hb, last8), 0)),
            pl.BlockSpec((1, t, 2 * LANES), lambda i, s: (i, cidx(s), 0)),
        ]

    prep_f = lambda s: jnp.minimum(s, nc - 1)
    prep_b = lambda s: jnp.maximum(nc - 1 - s, 0)
    run_f = lambda s: jnp.maximum(s - 1, 0)
    run_b = lambda s: jnp.minimum(nc - s, nc - 1)
    state_spec = pl.BlockSpec((1, 2, n_stack, MXU_DIM, MXU_DIM), lambda i, s: (i, 0, 0, 0, 0))
    in_specs = chunk_specs(prep_f) + chunk_specs(prep_b) + [
        pl.BlockSpec((8, c3), lambda i, s: (0, 0)),
        pl.BlockSpec((2, 8, rdim), lambda i, s: (0, 0, 0)),
        pl.BlockSpec((2, LANES, rdim), lambda i, s: (0, 0, 0)),
        pl.BlockSpec((2, LANES, rdim), lambda i, s: (0, 0, 0)),
        state_spec,
    ]
    out_specs, out_shapes = [], []
    if emit_y:
        out_specs += [
            pl.BlockSpec((1, t, rdim), lambda i, s: (i, run_f(s), 0)),
            pl.BlockSpec((1, t, rdim), lambda i, s: (i, run_b(s), 0)),
            pl.BlockSpec((1, t, rdim), lambda i, s: (i, prep_f(s), 0)),
        ]
        out_shapes += [jax.ShapeDtypeStruct((b, l, rdim), F32)] * 3
    out_specs.append(state_spec)
    out_shapes.append(jax.ShapeDtypeStruct((b, 2, n_stack, MXU_DIM, MXU_DIM), F32))
    n_mats = 9 if emit_y else 6
    return pl.pallas_call(
        functools.partial(_wkv3_kernel, n_chunks=nc, emit_y=emit_y),
        grid=(b, nc + 1),
        in_specs=in_specs,
        out_specs=out_specs,
        out_shape=out_shapes,
        scratch_shapes=[
            pltpu.VMEM((2, n_stack, MXU_DIM, MXU_DIM), F32),
            pltpu.VMEM((2 * n_stack, n_mats, MXU_DIM, MXU_DIM), BF16),
            pltpu.VMEM((2 * n_stack, MXU_DIM, MXU_DIM), F32),
            pltpu.VMEM((2 * n_stack, 8, MXU_DIM), F32),
        ],
        compiler_params=_params(("parallel", "arbitrary")),
        name="wkv",
    )(z_rkv, z_rkv, z_rkv, z_lora, z_rkv, z_rkv, z_rkv, z_lora, conv_w8, pvec, w2_pad, a2_pad, s0)


def _rope(x, cos, sin):
    lane = lax.broadcasted_iota(jnp.int32, x.shape, 1)
    first = (lane & 31) < 16
    partner = jnp.where(first, pltpu.roll(x, LANES - 16, 1), pltpu.roll(x, 16, 1))
    return x * cos + partner * sin


def _head_rmsnorm(x, g, ones_bd):
    ss = _head_sum(x * x, ones_bd)
    return x * lax.rsqrt(ss * (1.0 / HEAD_DIM) + NORM_EPS) * g


def _attn_kernel(q_ref, kv_ref, kvc_ref, cos_ref, sin_ref, qg_ref, kg_ref, sink_ref, o_ref,
                 kp_ref, vp_ref, kc_ref, vc_ref, *, seq_len, n_q_heads):
    blk = ATT_BLOCK
    i = pl.program_id(1)
    ones_bd = _head_ones(LANES)
    kg = kg_ref[...]
    n_kv = kv_ref.shape[-1] // (2 * HEAD_DIM)
    kw = n_kv * HEAD_DIM
    group = n_q_heads // n_kv
    lc = kvc_ref.shape[1]

    @pl.when(i == 0)
    def _():
        zeros = jnp.zeros((blk, kw), BF16)
        kp_ref[0:blk, :] = zeros
        vp_ref[0:blk, :] = zeros
        kp_ref[blk + seq_len:2 * blk + seq_len, :] = zeros
        vp_ref[blk + seq_len:2 * blk + seq_len, :] = zeros
        kk = _head_rmsnorm(kv_ref[0, :, 0:kw], kg, ones_bd)
        kp_ref[blk:blk + seq_len, :] = _rope(kk, cos_ref[...], sin_ref[...]).astype(BF16)
        vp_ref[blk:blk + seq_len, :] = kv_ref[0, :, kw:2 * kw].astype(BF16)
        kc_ref[...] = _head_rmsnorm(kvc_ref[0, :, 0:kw], kg, ones_bd).astype(BF16)
        vc_ref[...] = kvc_ref[0, :, kw:2 * kw].astype(BF16)

    start = pl.multiple_of(i * blk, blk)
    cos = cos_ref[pl.ds(start, blk), :]
    sin = sin_ref[pl.ds(start, blk), :]
    k_win = kp_ref[pl.ds(start, 3 * blk), :]
    v_win = vp_ref[pl.ds(start, 3 * blk), :]
    k_ctx = kc_ref[...]
    v_ctx = vc_ref[...]

    lane = lax.broadcasted_iota(jnp.int32, (blk, LANES), 1)
    low_half = lane < HEAD_DIM
    high_half = lane >= HEAD_DIM

    scale = HEAD_DIM ** -0.5
    q_heads = []
    for m in range(n_q_heads // 2):
        slab = q_ref[0, :, m * LANES:(m + 1) * LANES]
        slab = _rope(_head_rmsnorm(slab, qg_ref[...], ones_bd), cos, sin) * scale
        q_heads.append(slab)

    qr = lax.broadcasted_iota(jnp.int32, (blk, 3 * blk), 0)
    kc_i = lax.broadcasted_iota(jnp.int32, (blk, 3 * blk), 1)
    lo = jnp.maximum(qr, blk - start)
    hi = jnp.minimum(qr + 2 * WINDOW, seq_len + blk - 1 - start)
    hidden = jnp.where(kc_i < lo, 1.0, jnp.where(kc_i > hi, 1.0, 0.0))
    hidden = jnp.concatenate([hidden] * group, axis=0)

    heads = []
    for h in range(n_kv):
        kv_low = (h % 2) == 0
        parts = []
        for g in range(group):
            a = h * group + g
            slab = q_heads[a // 2]
            if ((a % 2) == 0) != kv_low:
                slab = pltpu.roll(slab, HEAD_DIM, 1)
            parts.append(jnp.where(low_half if kv_low else high_half, slab, 0.0))
        sl = slice((h // 2) * LANES, (h // 2 + 1) * LANES)
        heads.append(dict(
            kv_low=kv_low, qg=jnp.concatenate(parts, axis=0).astype(BF16),
            k_all=jnp.concatenate([k_win[:, sl], k_ctx[:, sl]], axis=0),
            v_all=jnp.concatenate([v_win[:, sl], v_ctx[:, sl]], axis=0),
            sink=jnp.concatenate([jnp.full((blk, 1), sink_ref[h * group + g], F32) for g in range(group)],
                                 axis=0)))
    for hd in heads:
        s_all = _dot_nt(hd["qg"], hd["k_all"])
        hd["s"] = jnp.concatenate(
            [jnp.where(hidden > 0.5, MASK_VALUE, s_all[:, 0:3 * blk]), s_all[:, 3 * blk:]], axis=1)
    for hd in heads:
        hd["mx"] = jnp.maximum(jnp.max(hd["s"], axis=-1, keepdims=True), hd["sink"])
    for hd in heads:
        hd["p"] = jnp.exp(hd["s"] - hd["mx"])
    for hd in heads:
        hd["den"] = jnp.sum(hd["p"], axis=-1, keepdims=True) + jnp.exp(hd["sink"] - hd["mx"])
    for hd in heads:
        hd["o"] = _dot(hd["p"].astype(BF16), hd["v_all"]) * (1.0 / hd["den"])
    out_heads = [None] * n_q_heads
    for h, hd in enumerate(heads):
        for g in range(group):
            a = h * group + g
            o_a = hd["o"][g * blk:(g + 1) * blk]
            if ((a % 2) == 0) != hd["kv_low"]:
                o_a = pltpu.roll(o_a, HEAD_DIM, 1)
            out_heads[a] = o_a
    for m in range(n_q_heads // 2):
        o_ref[0, :, m * LANES:(m + 1) * LANES] = jnp.where(low_half, out_heads[2 * m], out_heads[2 * m + 1])


def _attn(q, kv, kv_ctx, cos, sin, q_g, k_g, sink):
    b, l, qd = q.shape
    lc = kv_ctx.shape[1]
    kvd = kv.shape[-1]
    kw = kvd // 2
    n_q_heads = qd // HEAD_DIM
    blk = ATT_BLOCK
    return pl.pallas_call(
        functools.partial(_attn_kernel, seq_len=l, n_q_heads=n_q_heads),
        grid=(b, l // blk),
        in_specs=[
            pl.BlockSpec((1, blk, qd), lambda i, t: (i, t, 0)),
            pl.BlockSpec((1, l, kvd), lambda i, t: (i, 0, 0)),
            pl.BlockSpec((1, lc, kvd), lambda i, t: (i, 0, 0)),
            pl.BlockSpec((l, LANES), lambda i, t: (0, 0)),
            pl.BlockSpec((l, LANES), lambda i, t: (0, 0)),
            pl.BlockSpec((1, LANES), lambda i, t: (0, 0)),
            pl.BlockSpec((1, LANES), lambda i, t: (0, 0)),
            pl.BlockSpec(memory_space=pltpu.SMEM),
        ],
        out_specs=pl.BlockSpec((1, blk, qd), lambda i, t: (i, t, 0)),
        out_shape=jax.ShapeDtypeStruct((b, l, qd), F32),
        scratch_shapes=[
            pltpu.VMEM((l + 2 * blk, kw), BF16),
            pltpu.VMEM((l + 2 * blk, kw), BF16),
            pltpu.VMEM((lc, kw), BF16),
            pltpu.VMEM((lc, kw), BF16),
        ],
        compiler_params=_params(("parallel", "arbitrary")),
        name="attn",
    )(q, kv, kv_ctx, cos, sin, q_g, k_g, sink)


def _rope_tables(seq_len):
    tpos = jnp.arange(seq_len, dtype=jnp.int32)
    row = (tpos // GRID_W).astype(F32)
    col = (tpos % GRID_W).astype(F32)
    dim = jnp.arange(HEAD_DIM, dtype=jnp.int32)
    half = HEAD_DIM // 4
    inv_freq = ROPE_BASE ** (-(dim % half).astype(F32) / half)
    pos = jnp.where((dim // (HEAD_DIM // 2))[None, :] == 0, row[:, None], col[:, None])
    ang = pos * inv_freq[None, :]
    sign = jnp.where((dim % (HEAD_DIM // 2)) < half, -1.0, 1.0)[None, :]
    cos = jnp.cos(ang)
    sin = jnp.sin(ang) * sign
    reps = LANES // HEAD_DIM
    return jnp.tile(cos, (1, reps)), jnp.tile(sin, (1, reps))


def _merge_kernel(yf_ref, yb_ref, bo_ref, gd_ref, at_ref, gates_ref, x_ref, mod_ref, lnw_ref, lnb_ref, g2_ref,
                  wbr_ref, wba_ref, wo_ref, n2_ref, rw_ref, x1_ref, h2_ref, aff_ref):
    d_model = x_ref.shape[-1]
    ones_bd = _head_ones(MXU_DIM)
    rdim = yf_ref.shape[-1]
    ys = yf_ref[0] + yb_ref[0]
    bonus = bo_ref[0]
    cols = []
    for q in range(rdim // MXU_DIM):
        sl = slice(q * MXU_DIM, (q + 1) * MXU_DIM)
        yq = ys[:, sl]
        mean = _head_sum(yq, ones_bd) * (1.0 / HEAD_DIM)
        diff = yq - mean
        var = _head_sum(diff * diff, ones_bd) * (1.0 / HEAD_DIM)
        cols.append(diff * lax.rsqrt(var + LNX_EPS))
    yn = jnp.concatenate(cols, axis=1)
    y = yn * lnw_ref[...] + lnb_ref[...] + bonus
    g = _dot(_sigmoid(gd_ref[0]).astype(BF16), g2_ref[...])
    o_rwkv = (y * g).astype(BF16)
    br = _dot(o_rwkv, wbr_ref[...])
    ba = _dot(at_ref[0].astype(BF16), wba_ref[...])
    gates = gates_ref[0]
    merged = _sigmoid(gates[:, 0:d_model]) * br + _sigmoid(gates[:, d_model:2 * d_model]) * ba
    out = _dot(merged.astype(BF16), wo_ref[...])
    x1 = x_ref[0] + mod_ref[0, 2:3, :] * out
    x1_ref[0] = x1
    ms = jnp.mean(x1 * x1, axis=-1, keepdims=True)
    h2 = x1 * lax.rsqrt(ms + NORM_EPS) * n2_ref[...]
    h2 = h2 * (1.0 + mod_ref[0, 4:5, :]) + mod_ref[0, 3:4, :]
    h2_ref[0] = h2.astype(BF16)
    logits = lax.dot_general(rw_ref[...], h2, (((1,), (1,)), ((), ())),
                             precision=HIGHEST, preferred_element_type=F32)
    mx = jnp.max(logits, axis=0, keepdims=True)
    ex = jnp.exp(logits - mx)
    aff_ref[0] = ex / jnp.sum(ex, axis=0, keepdims=True)


def _merge(y_f, y_b, bonus, z_lora, at, gates, x, mod, lnx_w, lnx_b, g2, w_br_rwkv, w_br_attn, w_out,
           norm2_g, router_wt, tm):
    b, l, d = x.shape
    rdim = y_f.shape[-1]
    ne = router_wt.shape[0]
    const = lambda i, t: (0, 0)
    return pl.pallas_call(
        _merge_kernel,
        grid=(b, l // tm),
        in_specs=[
            pl.BlockSpec((1, tm, rdim), lambda i, t: (i, t, 0)),
            pl.BlockSpec((1, tm, rdim), lambda i, t: (i, t, 0)),
            pl.BlockSpec((1, tm, rdim), lambda i, t: (i, t, 0)),
            pl.BlockSpec((1, tm, LANES), lambda i, t: (i, t, 2)),
            pl.BlockSpec((1, tm, at.shape[-1]), lambda i, t: (i, t, 0)),
            pl.BlockSpec((1, tm, 2 * d), lambda i, t: (i, t, 0)),
            pl.BlockSpec((1, tm, d), lambda i, t: (i, t, 0)),
            pl.BlockSpec((1, 6, d), lambda i, t: (i, 0, 0)),
            pl.BlockSpec((1, rdim), const),
            pl.BlockSpec((1, rdim), const),
            pl.BlockSpec(g2.shape, const),
            pl.BlockSpec(w_br_rwkv.shape, const),
            pl.BlockSpec(w_br_attn.shape, const),
            pl.BlockSpec(w_out.shape, const),
            pl.BlockSpec((1, d), const),
            pl.BlockSpec(router_wt.shape, const),
        ],
        out_specs=[
            pl.BlockSpec((1, tm, d), lambda i, t: (i, t, 0)),
            pl.BlockSpec((1, tm, d), lambda i, t: (i, t, 0)),
            pl.BlockSpec((1, ne, tm), lambda i, t: (i, 0, t)),
        ],
        out_shape=[
            jax.ShapeDtypeStruct((b, l, d), F32),
            jax.ShapeDtypeStruct((b, l, d), BF16),
            jax.ShapeDtypeStruct((b, ne, l), F32),
        ],
        compiler_params=_params(("parallel", "parallel")),
        name="merge",
    )(y_f, y_b, bonus, z_lora, at, gates, x, mod, lnx_w.reshape(1, rdim), lnx_b.reshape(1, rdim), g2,
      w_br_rwkv, w_br_attn, w_out, norm2_g.reshape(1, d), router_wt)


def _prefix_count(mask_f, tri_excl):
    rows, l = mask_f.shape
    running = jnp.zeros((rows, 1), F32)
    pieces = []
    for blk in range(l // LANES):
        m = mask_f[:, blk * LANES:(blk + 1) * LANES]
        pieces.append(_dot(m.astype(BF16), tri_excl) + running)
        running = running + jnp.sum(m, axis=1, keepdims=True)
    return jnp.concatenate(pieces, axis=1)


def _route_kernel(aff_ref, sel_ref, *, cap):
    aff = aff_ref[0]
    bits = lax.bitcast_convert_type(aff, jnp.int32)
    thr = jnp.zeros((aff.shape[0], 1), jnp.int32)
    for bit in range(30, -1, -1):
        cand = thr | (1 << bit)
        cnt = jnp.sum(jnp.where(bits >= cand, 1.0, 0.0), axis=1, keepdims=True)
        thr = jnp.where(cnt >= cap, cand, thr)
    r = lax.broadcasted_iota(jnp.int32, (LANES, LANES), 0)
    c = lax.broadcasted_iota(jnp.int32, (LANES, LANES), 1)
    tri_excl = jnp.where(r < c, 1.0, 0.0).astype(BF16)
    gt = jnp.where(bits > thr, 1.0, 0.0)
    eq = jnp.where(bits == thr, 1.0, 0.0)
    need = cap - jnp.sum(gt, axis=1, keepdims=True)
    eq_rank = _prefix_count(eq, tri_excl)
    chosen = jnp.maximum(gt, jnp.where(eq_rank < need, eq, 0.0))
    rank = _prefix_count(chosen, tri_excl)
    sel_ref[0] = jnp.where(chosen > 0.5, rank, -1.0).astype(jnp.int32)


def _route(aff_t, cap):
    b, ne, l = aff_t.shape
    return pl.pallas_call(
        functools.partial(_route_kernel, cap=cap),
        grid=(b,),
        in_specs=[pl.BlockSpec((1, ne, l), lambda i: (i, 0, 0))],
        out_specs=pl.BlockSpec((1, ne, l), lambda i: (i, 0, 0)),
        out_shape=jax.ShapeDtypeStruct((b, ne, l), jnp.int32),
        compiler_params=_params(("parallel",)),
        name="route",
    )(aff_t)


def _moe_kernel(sel_ref, aff_ref, h_ref, x1_ref, mod_ref, wg_ref, wu_ref, wd_ref, o_ref, xe_ref, ye_ref, g_ref, *,
                cap, n_f, scatter_tile):
    e = pl.program_id(1)
    f = pl.program_id(2)
    l = h_ref.shape[1]

    @pl.when((e == 0) & (f == 0))
    def _():
        o_ref[...] = x1_ref[...]

    @pl.when(f == 0)
    def _():
        sel_row = sel_ref[0, pl.ds(e, 1), :]
        slot = lax.broadcasted_iota(jnp.int32, (cap, l), 0)
        onehot = jnp.where(slot == sel_row, 1.0, 0.0).astype(BF16)
        g_ref[...] = onehot
        xe_ref[...] = _dot(onehot, h_ref[0]).astype(BF16)
        ye_ref[...] = jnp.zeros(ye_ref.shape, F32)

    xe = xe_ref[...]
    hg = _dot(xe, wg_ref[0])
    hu = _dot(xe, wu_ref[0])
    act = (hg * _sigmoid(hg) * hu).astype(BF16)
    ye_ref[...] += _dot(act, wd_ref[0])

    @pl.when(f == n_f - 1)
    def _():
        sel_row = sel_ref[0, pl.ds(e, 1), :]
        aff_row = aff_ref[0, pl.ds(e, 1), :]
        slot = lax.broadcasted_iota(jnp.int32, (cap, l), 0)
        val = jnp.sum(jnp.where(slot == sel_row, aff_row, 0.0), axis=1, keepdims=True)
        yw = (ye_ref[...] * val * mod_ref[0, 5:6, :]).astype(BF16)
        for lt in range(l // scatter_tile):
            sl = slice(lt * scatter_tile, (lt + 1) * scatter_tile)
            o_ref[0, sl, :] += _dot_tn(g_ref[:, sl], yw)


def _moe(sel, aff_t, h2, x1, mod, wg, wu, wd, cap, n_f):
    b, l, d = h2.shape
    ne = sel.shape[1]
    ff = wg.shape[-1]
    fc = ff // n_f
    scatter_tile = min(l, 512)
    per_sample = lambda i, e, f: (i, 0, 0)
    return pl.pallas_call(
        functools.partial(_moe_kernel, cap=cap, n_f=n_f, scatter_tile=scatter_tile),
        grid=(b, ne, n_f),
        in_specs=[
            pl.BlockSpec((1, ne, l), per_sample),
            pl.BlockSpec((1, ne, l), per_sample),
            pl.BlockSpec((1, l, d), per_sample, pipeline_mode=pl.Buffered(1)),
            pl.BlockSpec((1, l, d), per_sample, pipeline_mode=pl.Buffered(1)),
            pl.BlockSpec((1, 6, d), per_sample),
            pl.BlockSpec((1, d, fc), lambda i, e, f: (e, 0, f)),
            pl.BlockSpec((1, d, fc), lambda i, e, f: (e, 0, f)),
            pl.BlockSpec((1, fc, d), lambda i, e, f: (e, f, 0)),
        ],
        out_specs=pl.BlockSpec((1, l, d), lambda i, e, f: (i, 0, 0)),
        out_shape=jax.ShapeDtypeStruct((b, l, d), F32),
        scratch_shapes=[
            pltpu.VMEM((cap, d), BF16),
            pltpu.VMEM((cap, d), F32),
            pltpu.VMEM((cap, l), BF16),
        ],
        compiler_params=_params(("parallel", "arbitrary", "arbitrary")),
        name="moe",
    )(sel, aff_t, h2, x1, mod, wg, wu, wd)


def _pad_rows(w, row0, rows):
    return jnp.zeros((rows, w.shape[1]), w.dtype).at[row0:row0 + w.shape[0]].set(w)


def _layer(x, ctx, mod_lat, mod_ctx, p):
    b, l, d = x.shape
    lc = ctx.shape[1]
    rdim = p["rwkv_w0"].shape[-1]
    dl = p["rwkv_w2"].shape[-2]
    al = p["rwkv_a2"].shape[-2]
    gl = p["rwkv_g2"].shape[0]
    att_dim = p["w_br_attn"].shape[0]
    kv_dim = (p["w_in"].shape[1] - 3 * rdim - 2 * dl - 2 * al - gl - att_dim - 2 * d) // 2
    assert 2 * dl == LANES and 2 * al == LANES and gl == LANES

    w_in = p["w_in"].astype(BF16)
    o = 0
    w_rkv = w_in[:, o:o + 3 * rdim]; o += 3 * rdim
    w_lora = w_in[:, o:o + 2 * dl + 2 * al + gl]; o += 2 * dl + 2 * al + gl
    w_q = w_in[:, o:o + att_dim]; o += att_dim
    w_kv = w_in[:, o:o + 2 * kv_dim]; o += 2 * kv_dim
    w_gates = w_in[:, o:o + 2 * d]

    tm = min(l, 512)
    z_rkv, z_lora, q, kv, gates = _proj(x, mod_lat, p["norm1_g"], [w_rkv, w_lora, w_q, w_kv, w_gates],
                                        min(l, 256), True)
    tmc = min(lc, 256)
    zc_rkv, zc_lora, kv_c = _proj(ctx, mod_ctx, p["norm1_g"], [w_rkv, w_lora, w_kv], tmc, False)

    conv_w8 = _pad_rows(p["rwkv_conv"], 0, 8)
    zeros3 = jnp.zeros((3, rdim), F32)
    pvec = jnp.stack([
        jnp.concatenate([p["rwkv_w0"][dd][None], p["rwkv_a0"][dd][None], p["rwkv_kk"][dd][None],
                         p["rwkv_ka"][dd][None], p["rwkv_rk"][dd].reshape(1, rdim), zeros3], axis=0)
        for dd in range(2)])
    w2_pad = jnp.stack([_pad_rows(p["rwkv_w2"][dd], dd * dl, LANES) for dd in range(2)]).astype(BF16)
    a2_pad = jnp.stack([_pad_rows(p["rwkv_a2"][dd], dd * al, LANES) for dd in range(2)]).astype(BF16)
    n_stack = rdim // MXU_DIM
    s0 = jnp.zeros((b, 2, n_stack, MXU_DIM, MXU_DIM), F32)
    (s_ctx,) = _wkv2(zc_rkv, zc_lora, conv_w8, pvec, w2_pad, a2_pad, s0, False)
    y_f, y_b, bonus, _ = _wkv2(z_rkv, z_lora, conv_w8, pvec, w2_pad, a2_pad, s_ctx, True)

    cos, sin = _rope_tables(l)
    reps = LANES // HEAD_DIM
    q_g = jnp.tile(p["q_norm_g"], reps).reshape(1, LANES)
    k_g = jnp.tile(p["k_norm_g"], reps).reshape(1, LANES)
    at = _attn(q, kv, kv_c, cos, sin, q_g, k_g, p["attn_sink"])

    x1, h2, aff_t = _merge(y_f, y_b, bonus, z_lora, at, gates, x, mod_lat, p["lnx_w"], p["lnx_b"],
                           p["rwkv_g2"].astype(BF16), p["w_br_rwkv"].astype(BF16),
                           p["w_br_attn"].astype(BF16), p["w_out"].astype(BF16), p["norm2_g"],
                           p["router_w"].T, tm)
    ne = p["router_w"].shape[1]
    cap = EC_FACTOR * l // ne
    sel = _route(aff_t, cap)
    n_f = 2 if p["exp_w_gate"].shape[-1] % (2 * MXU_DIM) == 0 else 1
    return _moe(sel, aff_t, h2, x1, mod_lat, p["exp_w_gate"].astype(BF16), p["exp_w_up"].astype(BF16),
                p["exp_w_down"].astype(BF16), cap, n_f)


def kernel(x, c, ctx, c_ctx, ada_w, ada_b, norm1_g, norm2_g, w_in, rwkv_conv, rwkv_w0, rwkv_w2, rwkv_a0, rwkv_a2, rwkv_kk, rwkv_ka, rwkv_rk, rwkv_g2, lnx_w, lnx_b, q_norm_g, k_norm_g, attn_sink, w_br_rwkv, w_br_attn, w_out, router_w, exp_w_gate, exp_w_up, exp_w_down):
    b, l, d = x.shape
    depth = ada_w.shape[0]
    assert depth == 1, "the context stream is only read, never advanced, for a single layer"
    rows = -(-(b + 1) // 8) * 8
    c_all = jnp.zeros((rows, d), F32).at[:b].set(c).at[b].set(c_ctx)
    names = ["ada_w", "ada_b", "norm1_g", "norm2_g", "w_in", "rwkv_conv", "rwkv_w0", "rwkv_w2", "rwkv_a0",
             "rwkv_a2", "rwkv_kk", "rwkv_ka", "rwkv_rk", "rwkv_g2", "lnx_w", "lnx_b", "q_norm_g", "k_norm_g",
             "attn_sink", "w_br_rwkv", "w_br_attn", "w_out", "router_w", "exp_w_gate", "exp_w_up", "exp_w_down"]
    vals = [ada_w, ada_b, norm1_g, norm2_g, w_in, rwkv_conv, rwkv_w0, rwkv_w2, rwkv_a0, rwkv_a2, rwkv_kk,
            rwkv_ka, rwkv_rk, rwkv_g2, lnx_w, lnx_b, q_norm_g, k_norm_g, attn_sink, w_br_rwkv, w_br_attn,
            w_out, router_w, exp_w_gate, exp_w_up, exp_w_down]
    p = {n: v[0] for n, v in zip(names, vals)}
    mod = _adaln(c_all, p["ada_w"], p["ada_b"])
    mod_lat = mod[:b].reshape(b, 6, d)
    mod_ctx = mod[b:b + 1].reshape(1, 6, d)
    return _layer(x, ctx, mod_lat, mod_ctx, p)
```

```python
import functools

import jax
import jax.numpy as jnp
from jax import lax
from jax.experimental import pallas as pl
from jax.experimental.pallas import tpu as pltpu

F32 = jnp.float32
BF16 = jnp.bfloat16
HIGHEST = lax.Precision.HIGHEST

HEAD_DIM = 64
HEAD_SHIFT = 6
GRID_W = 64
WINDOW = 128
ATT_BLOCK = 128
ATT_Q_BLOCKS_PER_STEP = 4
ROPE_BASE = 10000.0
NORM_EPS = 1e-6
LNX_EPS = 64e-5
MASK_VALUE = -1e30
EC_FACTOR = 2

LANES = 128
MXU_DIM = 256
VMEM_LIMIT_BYTES = 56 * 1024 * 1024

WKV_CHUNK = 64
WKV_SAMPLES_PER_STEP = 4
HEADS_PER_STACK = MXU_DIM // HEAD_DIM


def _dot(a, b):
    return jnp.dot(a, b, preferred_element_type=F32)


def _dot_nt(a, b):
    return lax.dot_general(a, b, (((1,), (1,)), ((), ())), preferred_element_type=F32)


def _dot_tn(a, b):
    return lax.dot_general(a, b, (((0,), (0,)), ((), ())), preferred_element_type=F32)


def _sigmoid(x):
    return 1.0 / (1.0 + jnp.exp(-x))


def _split_bf16(x, parts):
    out = []
    rem = x
    for _ in range(parts):
        hi = rem.astype(BF16)
        out.append(hi)
        rem = rem - hi.astype(F32)
    return out


def _dot_split_rhs(a_bf16, x, parts):
    acc = None
    for term in _split_bf16(x, parts):
        d = _dot(a_bf16, term)
        acc = d if acc is None else acc + d
    return acc


def _dot_split_lhs(x, b_bf16, parts):
    acc = None
    for term in _split_bf16(x, parts):
        d = _dot(term, b_bf16)
        acc = d if acc is None else acc + d
    return acc


def _head_ones(width):
    r = lax.broadcasted_iota(jnp.int32, (width, width), 0) >> HEAD_SHIFT
    c = lax.broadcasted_iota(jnp.int32, (width, width), 1) >> HEAD_SHIFT
    return jnp.where(r == c, 1.0, 0.0).astype(BF16)


def _head_sum(x, ones_bd):
    return _dot_split_lhs(x, ones_bd, 2)


def _params(semantics):
    return pltpu.CompilerParams(dimension_semantics=semantics, vmem_limit_bytes=VMEM_LIMIT_BYTES)


def _adaln_kernel(c_ref, w_ref, b_ref, o_ref):
    c = c_ref[...]
    s = c * _sigmoid(c)
    o_ref[...] = jnp.dot(s, w_ref[...], precision=HIGHEST, preferred_element_type=F32) + b_ref[...]


def _adaln(c_all, ada_w, ada_b):
    rows, d = c_all.shape
    n = ada_w.shape[1]
    tn = 512
    return pl.pallas_call(
        _adaln_kernel,
        grid=(n // tn,),
        in_specs=[
            pl.BlockSpec((rows, d), lambda j: (0, 0)),
            pl.BlockSpec((d, tn), lambda j: (0, j)),
            pl.BlockSpec((1, tn), lambda j: (0, j)),
        ],
        out_specs=pl.BlockSpec((rows, tn), lambda j: (0, j)),
        out_shape=jax.ShapeDtypeStruct((rows, n), F32),
        compiler_params=_params(("parallel",)),
        name="adaln",
    )(c_all, ada_w, ada_b.reshape(1, n))


def _proj_kernel(x_ref, mod_ref, g_ref, *refs):
    n_out = len(refs) // 2
    w_refs, o_refs = refs[:n_out], refs[n_out:]
    x = x_ref[0]
    ms = jnp.mean(x * x, axis=-1, keepdims=True)
    y = x * lax.rsqrt(ms + NORM_EPS) * g_ref[...]
    shift = mod_ref[0, 0:1, :]
    scale = mod_ref[0, 1:2, :]
    h = (y * (1.0 + scale) + shift).astype(BF16)
    for w_ref, o_ref in zip(w_refs, o_refs):
        o_ref[0] = _dot(h, w_ref[...])


def _proj(x, mod, norm_g, weights, tm, per_sample_mod):
    b, l, d = x.shape
    mod_map = (lambda i, t: (i, 0, 0)) if per_sample_mod else (lambda i, t: (0, 0, 0))
    in_specs = [
        pl.BlockSpec((1, tm, d), lambda i, t: (i, t, 0)),
        pl.BlockSpec((1, 6, d), mod_map),
        pl.BlockSpec((1, d), lambda i, t: (0, 0)),
    ]
    out_specs, out_shapes = [], []
    for w in weights:
        n = w.shape[1]
        in_specs.append(pl.BlockSpec((d, n), lambda i, t: (0, 0)))
        out_specs.append(pl.BlockSpec((1, tm, n), lambda i, t: (i, t, 0)))
        out_shapes.append(jax.ShapeDtypeStruct((b, l, n), F32))
    return pl.pallas_call(
        _proj_kernel,
        grid=(b, l // tm),
        in_specs=in_specs,
        out_specs=out_specs,
        out_shape=out_shapes,
        compiler_params=_params(("parallel", "parallel")),
        name="proj",
    )(x, mod, norm_g.reshape(1, d), *weights)


def _wkv_kernel(*refs, n_chunks, emit_y):
    t = WKV_CHUNK
    (zcf, zpf, znf, zlf, zcb, zpb, znb, zlb, cw_ref, pv_ref, w2_ref, a2_ref, s0_ref) = refs[:13]
    if emit_y:
        yf_ref, yb_ref, bo_ref, s1_ref, s_ref = refs[13:]
    else:
        s1_ref, s_ref = refs[13:]
    rdim = pv_ref.shape[-1]
    n_stack = rdim // MXU_DIM
    n_samples = zcf.shape[0]
    j = pl.program_id(1)

    @pl.when(j == 0)
    def _():
        s_ref[...] = s0_ref[...]

    ones_bd = _head_ones(MXU_DIM)
    sr = lax.broadcasted_iota(jnp.int32, (MXU_DIM, MXU_DIM), 0)
    sc = lax.broadcasted_iota(jnp.int32, (MXU_DIM, MXU_DIM), 1)
    stack_mask = jnp.where((sr >> HEAD_SHIFT) == (sc >> HEAD_SHIFT), 1.0, 0.0).astype(BF16)
    eye_b = jnp.where(sr == sc, 1.0, 0.0).astype(BF16)
    tr = lax.broadcasted_iota(jnp.int32, (t, t), 0)
    tc = lax.broadcasted_iota(jnp.int32, (t, t), 1)
    row = lax.broadcasted_iota(jnp.int32, (t, MXU_DIM), 0)
    head_mask = jnp.where((sr >> HEAD_SHIFT) == (sc >> HEAD_SHIFT), 1.0, 0.0)

    def stack(x):
        xb = x.astype(BF16)
        return jnp.concatenate([xb] * HEADS_PER_STACK, axis=0) * stack_mask

    def prepare(bi, chains):
        for dd in range(2):
            rev = dd == 1
            chunk = (n_chunks - 1 - j) if rev else j
            zc, zp, zn, zl = (zcb, zpb, znb, zlb) if rev else (zcf, zpf, znf, zlf)
            has_prev = chunk > 0
            has_next = chunk < n_chunks - 1

            def conv(c0):
                cs = slice(c0, c0 + MXU_DIM)
                zm = zc[bi, :, cs]
                prev_row = jnp.where(has_prev, zp[bi, 7:8, cs], 0.0)
                next_row = jnp.where(has_next, zn[bi, 0:1, cs], 0.0)
                z_up = jnp.where(row == 0, prev_row, pltpu.roll(zm, 1, 0))
                z_dn = jnp.where(row == t - 1, next_row, pltpu.roll(zm, t - 1, 0))
                return cw_ref[0:1, cs] * z_up + cw_ref[1:2, cs] * zm + cw_ref[2:3, cs] * z_dn

            wd = zl[bi, :, 0:LANES]
            ad = zl[bi, :, LANES:2 * LANES].astype(BF16)
            wl = pv_ref[dd, 0:1, :] + _dot(jnp.tanh(wd).astype(BF16), w2_ref[dd])
            yield
            neg = -wl
            softplus = jnp.maximum(neg, 0.0) + jnp.log(1.0 + jnp.exp(-jnp.abs(neg)))
            lw = -jnp.exp(-softplus - 0.5)
            yield
            asig = _sigmoid(pv_ref[dd, 1:2, :] + _dot(ad, a2_ref[dd]))
            yield
            if emit_y and not rev:
                asig_o = _sigmoid(pv_ref[1, 1:2, :] + _dot(ad, a2_ref[1]))
                yield
            tri = jnp.where((tr <= tc) if rev else (tr >= tc), 1.0, 0.0).astype(BF16)
            cum = _dot_split_rhs(tri, lw, 3)
            yield
            c_end = cum[0:1, :] if rev else cum[t - 1:t, :]
            rho = 0.5 * c_end
            e_in = jnp.exp(cum - rho)
            yield
            e_out = jnp.exp(rho - cum)
            yield
            e_ex = jnp.exp(cum - lw - rho)
            e_rho = jnp.exp(rho)
            e_end = jnp.exp(c_end)
            yield
            strict = (sr < sc) if rev else (sr > sc)
            incl = (sr <= sc) if rev else (sr >= sc)

            for q in range(n_stack):
                sl = slice(q * MXU_DIM, (q + 1) * MXU_DIM)
                ch = dict(bi=bi, dd=dd, q=q, sl=sl, rev=rev, e_rho=e_rho[:, sl], e_end=e_end[:, sl])
                k_q = conv(rdim + q * MXU_DIM)
                yield
                v_q = conv(2 * rdim + q * MXU_DIM)
                yield
                kk_q = k_q * pv_ref[dd, 2:3, sl]
                ss = _head_sum(kk_q * kk_q, ones_bd)
                kkn = kk_q * lax.rsqrt(jnp.maximum(ss, 1e-24))
                yield
                k_mod = k_q * (1.0 + (asig[:, sl] - 1.0) * pv_ref[dd, 3:4, sl])
                a_u = (-kkn * e_ex[:, sl]).astype(BF16)
                b_u = (kkn * asig[:, sl] * e_out[:, sl]).astype(BF16)
                k_u = (k_mod * e_out[:, sl]).astype(BF16)
                v_u = v_q.astype(BF16)
                ch["v_u"] = v_u
                ch["bk_u"] = jnp.concatenate([b_u, k_u], axis=0)
                ch["a_st"] = stack(a_u)
                yield
                ch["b_st"] = stack(b_u)
                yield
                ch["k_st"] = stack(k_u)
                yield
                ch["v_st"] = stack(v_u)
                yield
                if emit_y:
                    r_q = conv(q * MXU_DIM)
                    yield
                    r_u = (r_q * e_in[:, sl]).astype(BF16)
                    ch["ar_u"] = jnp.concatenate([a_u, r_u], axis=0)
                    ch["ar_st"] = jnp.concatenate([ch["a_st"], stack(r_u)], axis=0)
                    yield
                    nb = _dot_nt(ch["ar_st"], ch["b_st"])
                    yield
                    nk = _dot_nt(ch["ar_st"], ch["k_st"])
                    yield
                    ch["a_rb"] = jnp.where(incl, nb[MXU_DIM:], 0.0).astype(BF16)
                    yield
                    ak = jnp.where(strict, nk[:MXU_DIM], 0.0).astype(BF16)
                    yield
                    ch["akrk"] = jnp.concatenate([ak, jnp.where(incl, nk[MXU_DIM:], 0.0).astype(BF16)], axis=0)
                    yield
                    if not rev:
                        k_mod_o = k_q * (1.0 + (asig_o[:, sl] - 1.0) * pv_ref[1, 3:4, sl])
                        rkk = r_q * (k_mod * pv_ref[0, 4:5, sl] + k_mod_o * pv_ref[1, 4:5, sl])
                        bo_ref[bi, :, sl] = _head_sum(rkk, ones_bd) * v_q
                        yield
                else:
                    ch["ar_u"] = a_u
                    nb = _dot_nt(ch["a_st"], ch["b_st"])
                    yield
                    ch["akrk"] = jnp.where(strict, _dot_nt(ch["a_st"], ch["k_st"]), 0.0).astype(BF16)
                    yield
                ch["p"] = jnp.where(strict, nb[:MXU_DIM], 0.0).astype(BF16)
                ch["inv"] = ch["p"] + eye_b
                chains.append(ch)
                yield

    def advance(chains, filler):
        def tick():
            if filler is not None:
                next(filler, None)

        for _ in range(t.bit_length() - 2):
            for ch in chains:
                ch["p"] = _dot(ch["p"], ch["p"]).astype(BF16)
                tick()
            for ch in chains:
                ch["inv"] = _dot(ch["inv"], ch["p"] + eye_b).astype(BF16)
                tick()
        def unstack(m):
            out = m[0:t]
            for hh in range(1, HEADS_PER_STACK):
                out = out + m[hh * t:(hh + 1) * t]
            return out

        for ch in chains:
            ch["s_q"] = s_ref[ch["bi"], ch["dd"], ch["q"]]
            ch["s_rho"] = (ch["s_q"] * ch["e_rho"]).astype(BF16)
            ch["xs"] = _dot_nt(ch["ar_u"], ch["s_rho"])
            tick()
            ch["xv"] = _dot(ch["akrk"], ch["v_st"])
            tick()
            xs_st = jnp.concatenate([ch["xs"][0:t]] * HEADS_PER_STACK, axis=0) * head_mask
            ch["x"] = (xs_st + ch["xv"][:MXU_DIM]).astype(BF16)
        for ch in chains:
            z32 = _dot(ch["inv"], ch["x"])
            tick()
            ch["z"] = z32.astype(BF16)
            ch["zv_u"] = jnp.concatenate([unstack(z32).astype(BF16), ch["v_u"]], axis=0)
        for ch in chains:
            upd = _dot_tn(ch["zv_u"], ch["bk_u"]) * head_mask
            tick()
            s_ref[ch["bi"], ch["dd"], ch["q"]] = ch["s_q"] * ch["e_end"] + upd * ch["e_rho"]
        if emit_y:
            for ch in chains:
                y_mat = ch["xv"][MXU_DIM:] + _dot(ch["a_rb"], ch["z"])
                tick()
                (yb_ref if ch["rev"] else yf_ref)[ch["bi"], :, ch["sl"]] = ch["xs"][t:2 * t] + unstack(y_mat)

    chain_lists = [[] for _ in range(n_samples)]
    preps = [prepare(bi, chain_lists[bi]) for bi in range(n_samples)]
    for _ in preps[0]:
        pass
    for bi in range(n_samples):
        filler = preps[bi + 1] if bi + 1 < n_samples else None
        advance(chain_lists[bi], filler)
        if filler is not None:
            for _ in filler:
                pass

    @pl.when(j == n_chunks - 1)
    def _():
        s1_ref[...] = s_ref[...]


def _wkv(z_rkv, z_lora, conv_w8, pvec, w2_pad, a2_pad, s0, emit_y):
    b, l, c3 = z_rkv.shape
    rdim = c3 // 3
    t = WKV_CHUNK
    nc = l // t
    n_stack = rdim // MXU_DIM
    hb = t // 8
    last8 = l // 8 - 1
    bs = WKV_SAMPLES_PER_STEP if b % WKV_SAMPLES_PER_STEP == 0 else 1

    def chunk_specs(cidx):
        return [
            pl.BlockSpec((bs, t, c3), lambda i, jj: (i, cidx(jj), 0)),
            pl.BlockSpec((bs, 8, c3), lambda i, jj: (i, jnp.maximum(cidx(jj) * hb - 1, 0), 0)),
            pl.BlockSpec((bs, 8, c3), lambda i, jj: (i, jnp.minimum((cidx(jj) + 1) * hb, last8), 0)),
            pl.BlockSpec((bs, t, 2 * LANES), lambda i, jj: (i, cidx(jj), 0)),
        ]

    fwd = lambda jj: jj
    bwd = lambda jj: nc - 1 - jj
    state_spec = pl.BlockSpec((bs, 2, n_stack, MXU_DIM, MXU_DIM), lambda i, jj: (i, 0, 0, 0, 0))
    in_specs = chunk_specs(fwd) + chunk_specs(bwd) + [
        pl.BlockSpec((8, c3), lambda i, jj: (0, 0)),
        pl.BlockSpec((2, 8, rdim), lambda i, jj: (0, 0, 0)),
        pl.BlockSpec((2, LANES, rdim), lambda i, jj: (0, 0, 0)),
        pl.BlockSpec((2, LANES, rdim), lambda i, jj: (0, 0, 0)),
        state_spec,
    ]
    out_specs, out_shapes = [], []
    if emit_y:
        out_specs += [
            pl.BlockSpec((bs, t, rdim), lambda i, jj: (i, fwd(jj), 0)),
            pl.BlockSpec((bs, t, rdim), lambda i, jj: (i, bwd(jj), 0)),
            pl.BlockSpec((bs, t, rdim), lambda i, jj: (i, fwd(jj), 0)),
        ]
        out_shapes += [jax.ShapeDtypeStruct((b, l, rdim), F32)] * 3
    out_specs.append(state_spec)
    out_shapes.append(jax.ShapeDtypeStruct((b, 2, n_stack, MXU_DIM, MXU_DIM), F32))
    return pl.pallas_call(
        functools.partial(_wkv_kernel, n_chunks=nc, emit_y=emit_y),
        grid=(b // bs, nc),
        in_specs=in_specs,
        out_specs=out_specs,
        out_shape=out_shapes,
        scratch_shapes=[pltpu.VMEM((bs, 2, n_stack, MXU_DIM, MXU_DIM), F32)],
        compiler_params=_params(("parallel", "arbitrary")),
        name="wkv",
    )(*([z_rkv, z_rkv, z_rkv, z_lora] * 2), conv_w8, pvec, w2_pad, a2_pad, s0)


def _rope(x, cos, sin):
    lane = lax.broadcasted_iota(jnp.int32, x.shape, 1)
    first = (lane & 31) < 16
    partner = jnp.where(first, pltpu.roll(x, LANES - 16, 1), pltpu.roll(x, 16, 1))
    return x * cos + partner * sin


def _head_rmsnorm(x, g, ones_bd):
    ss = _head_sum(x * x, ones_bd)
    return x * lax.rsqrt(ss * (1.0 / HEAD_DIM) + NORM_EPS) * g


def _attn_kernel(q_ref, kv_ref, kvc_ref, cos_ref, sin_ref, qg_ref, kg_ref, sink_ref, o_ref,
                 kp_ref, vp_ref, kc_ref, vc_ref, *, seq_len, n_q_heads):
    blk = ATT_BLOCK
    i = pl.program_id(1)
    ones_bd = _head_ones(LANES)
    kg = kg_ref[...]
    n_kv = kv_ref.shape[-1] // (2 * HEAD_DIM)
    kw = n_kv * HEAD_DIM
    group = n_q_heads // n_kv
    lc = kvc_ref.shape[1]

    @pl.when(i == 0)
    def _():
        zeros = jnp.zeros((blk, kw), BF16)
        kp_ref[0:blk, :] = zeros
        vp_ref[0:blk, :] = zeros
        kp_ref[blk + seq_len:2 * blk + seq_len, :] = zeros
        vp_ref[blk + seq_len:2 * blk + seq_len, :] = zeros
        kk = _head_rmsnorm(kv_ref[0, :, 0:kw], kg, ones_bd)
        kp_ref[blk:blk + seq_len, :] = _rope(kk, cos_ref[...], sin_ref[...]).astype(BF16)
        vp_ref[blk:blk + seq_len, :] = kv_ref[0, :, kw:2 * kw].astype(BF16)
        kc_ref[...] = _head_rmsnorm(kvc_ref[0, :, 0:kw], kg, ones_bd).astype(BF16)
        vc_ref[...] = kvc_ref[0, :, kw:2 * kw].astype(BF16)

    k_ctx = kc_ref[...]
    v_ctx = vc_ref[...]
    lane = lax.broadcasted_iota(jnp.int32, (blk, LANES), 1)
    low_half = lane < HEAD_DIM
    high_half = lane >= HEAD_DIM
    scale = HEAD_DIM ** -0.5
    qr = lax.broadcasted_iota(jnp.int32, (blk, 3 * blk), 0)
    kc_i = lax.broadcasted_iota(jnp.int32, (blk, 3 * blk), 1)
    n_sub = q_ref.shape[1] // blk

    heads = []
    for sb in range(n_sub):
        start = pl.multiple_of((i * n_sub + sb) * blk, blk)
        rows = slice(sb * blk, (sb + 1) * blk)
        cos = cos_ref[pl.ds(start, blk), :]
        sin = sin_ref[pl.ds(start, blk), :]
        k_win = kp_ref[pl.ds(start, 3 * blk), :]
        v_win = vp_ref[pl.ds(start, 3 * blk), :]

        q_heads = []
        for m in range(n_q_heads // 2):
            slab = q_ref[0, rows, m * LANES:(m + 1) * LANES]
            slab = _rope(_head_rmsnorm(slab, qg_ref[...], ones_bd), cos, sin) * scale
            q_heads.append(slab)

        lo = jnp.maximum(qr, blk - start)
        hi = jnp.minimum(qr + 2 * WINDOW, seq_len + blk - 1 - start)
        hidden = jnp.where(kc_i < lo, 1.0, jnp.where(kc_i > hi, 1.0, 0.0))
        hidden = jnp.concatenate([hidden] * group, axis=0)

        for h in range(n_kv):
            kv_low = (h % 2) == 0
            parts = []
            for g in range(group):
                a = h * group + g
                slab = q_heads[a // 2]
                if ((a % 2) == 0) != kv_low:
                    slab = pltpu.roll(slab, HEAD_DIM, 1)
                parts.append(jnp.where(low_half if kv_low else high_half, slab, 0.0))
            sl = slice((h // 2) * LANES, (h // 2 + 1) * LANES)
            heads.append(dict(
                h=h, rows=rows, kv_low=kv_low, hidden=hidden,
                qg=jnp.concatenate(parts, axis=0).astype(BF16),
                k_all=jnp.concatenate([k_win[:, sl], k_ctx[:, sl]], axis=0),
                v_all=jnp.concatenate([v_win[:, sl], v_ctx[:, sl]], axis=0),
                sink=jnp.concatenate([jnp.full((blk, 1), sink_ref[h * group + g], F32) for g in range(group)],
                                     axis=0)))
    for hd in heads:
        s_all = _dot_nt(hd["qg"], hd["k_all"])
        hd["s"] = jnp.concatenate(
            [jnp.where(hd["hidden"] > 0.5, MASK_VALUE, s_all[:, 0:3 * blk]), s_all[:, 3 * blk:]], axis=1)
    for hd in heads:
        hd["mx"] = jnp.maximum(jnp.max(hd["s"], axis=-1, keepdims=True), hd["sink"])
    for hd in heads:
        hd["p"] = jnp.exp(hd["s"] - hd["mx"])
    for hd in heads:
        hd["den"] = jnp.sum(hd["p"], axis=-1, keepdims=True) + jnp.exp(hd["sink"] - hd["mx"])
    for hd in heads:
        hd["o"] = _dot(hd["p"].astype(BF16), hd["v_all"]) * (1.0 / hd["den"])
    for sb in range(n_sub):
        out_heads = [None] * n_q_heads
        for hd in heads[sb * n_kv:(sb + 1) * n_kv]:
            for g in range(group):
                a = hd["h"] * group + g
                o_a = hd["o"][g * blk:(g + 1) * blk]
                if ((a % 2) == 0) != hd["kv_low"]:
                    o_a = pltpu.roll(o_a, HEAD_DIM, 1)
                out_heads[a] = o_a
        rows = slice(sb * blk, (sb + 1) * blk)
        for m in range(n_q_heads // 2):
            o_ref[0, rows, m * LANES:(m + 1) * LANES] = jnp.where(low_half, out_heads[2 * m], out_heads[2 * m + 1])


def _attn(q, kv, kv_ctx, cos, sin, q_g, k_g, sink):
    b, l, qd = q.shape
    lc = kv_ctx.shape[1]
    kvd = kv.shape[-1]
    kw = kvd // 2
    n_q_heads = qd // HEAD_DIM
    blk = ATT_BLOCK
    tq = ATT_Q_BLOCKS_PER_STEP * blk if l % (ATT_Q_BLOCKS_PER_STEP * blk) == 0 else blk
    return pl.pallas_call(
        functools.partial(_attn_kernel, seq_len=l, n_q_heads=n_q_heads),
        grid=(b, l // tq),
        in_specs=[
            pl.BlockSpec((1, tq, qd), lambda i, t: (i, t, 0)),
            pl.BlockSpec((1, l, kvd), lambda i, t: (i, 0, 0)),
            pl.BlockSpec((1, lc, kvd), lambda i, t: (i, 0, 0)),
            pl.BlockSpec((l, LANES), lambda i, t: (0, 0)),
            pl.BlockSpec((l, LANES), lambda i, t: (0, 0)),
            pl.BlockSpec((1, LANES), lambda i, t: (0, 0)),
            pl.BlockSpec((1, LANES), lambda i, t: (0, 0)),
            pl.BlockSpec(memory_space=pltpu.SMEM),
        ],
        out_specs=pl.BlockSpec((1, tq, qd), lambda i, t: (i, t, 0)),
        out_shape=jax.ShapeDtypeStruct((b, l, qd), F32),
        scratch_shapes=[
            pltpu.VMEM((l + 2 * blk, kw), BF16),
            pltpu.VMEM((l + 2 * blk, kw), BF16),
            pltpu.VMEM((lc, kw), BF16),
            pltpu.VMEM((lc, kw), BF16),
        ],
        compiler_params=_params(("parallel", "arbitrary")),
        name="attn",
    )(q, kv, kv_ctx, cos, sin, q_g, k_g, sink)


def _rope_tables(seq_len):
    tpos = jnp.arange(seq_len, dtype=jnp.int32)
    row = (tpos // GRID_W).astype(F32)
    col = (tpos % GRID_W).astype(F32)
    dim = jnp.arange(HEAD_DIM, dtype=jnp.int32)
    half = HEAD_DIM // 4
    inv_freq = ROPE_BASE ** (-(dim % half).astype(F32) / half)
    pos = jnp.where((dim // (HEAD_DIM // 2))[None, :] == 0, row[:, None], col[:, None])
    ang = pos * inv_freq[None, :]
    sign = jnp.where((dim % (HEAD_DIM // 2)) < half, -1.0, 1.0)[None, :]
    cos = jnp.cos(ang)
    sin = jnp.sin(ang) * sign
    reps = LANES // HEAD_DIM
    return jnp.tile(cos, (1, reps)), jnp.tile(sin, (1, reps))


def _merge_kernel(yf_ref, yb_ref, bo_ref, gd_ref, at_ref, gates_ref, x_ref, mod_ref, lnw_ref, lnb_ref, g2_ref,
                  wbr_ref, wba_ref, wo_ref, n2_ref, rw_ref, x1_ref, h2_ref, aff_ref):
    d_model = x_ref.shape[-1]
    ones_bd = _head_ones(MXU_DIM)
    rdim = yf_ref.shape[-1]
    ys = yf_ref[0] + yb_ref[0]
    bonus = bo_ref[0]
    cols = []
    for q in range(rdim // MXU_DIM):
        sl = slice(q * MXU_DIM, (q + 1) * MXU_DIM)
        yq = ys[:, sl]
        mean = _head_sum(yq, ones_bd) * (1.0 / HEAD_DIM)
        diff = yq - mean
        var = _head_sum(diff * diff, ones_bd) * (1.0 / HEAD_DIM)
        cols.append(diff * lax.rsqrt(var + LNX_EPS))
    yn = jnp.concatenate(cols, axis=1)
    y = yn * lnw_ref[...] + lnb_ref[...] + bonus
    g = _dot(_sigmoid(gd_ref[0]).astype(BF16), g2_ref[...])
    o_rwkv = (y * g).astype(BF16)
    br = _dot(o_rwkv, wbr_ref[...])
    ba = _dot(at_ref[0].astype(BF16), wba_ref[...])
    gates = gates_ref[0]
    merged = _sigmoid(gates[:, 0:d_model]) * br + _sigmoid(gates[:, d_model:2 * d_model]) * ba
    out = _dot(merged.astype(BF16), wo_ref[...])
    x1 = x_ref[0] + mod_ref[0, 2:3, :] * out
    x1_ref[0] = x1
    ms = jnp.mean(x1 * x1, axis=-1, keepdims=True)
    h2 = x1 * lax.rsqrt(ms + NORM_EPS) * n2_ref[...]
    h2 = h2 * (1.0 + mod_ref[0, 4:5, :]) + mod_ref[0, 3:4, :]
    h_hi, h_lo = _split_bf16(h2, 2)
    h2_ref[0] = h_hi
    w_hi, w_lo = _split_bf16(rw_ref[...], 2)
    logits = _dot_nt(w_hi, h_hi) + _dot_nt(w_lo, h_hi) + _dot_nt(w_hi, h_lo)
    mx = jnp.max(logits, axis=0, keepdims=True)
    ex = jnp.exp(logits - mx)
    aff_ref[0] = ex / jnp.sum(ex, axis=0, keepdims=True)


def _merge(y_f, y_b, bonus, z_lora, at, gates, x, mod, lnx_w, lnx_b, g2, w_br_rwkv, w_br_attn, w_out,
           norm2_g, router_wt, tm):
    b, l, d = x.shape
    rdim = y_f.shape[-1]
    ne = router_wt.shape[0]
    const = lambda i, t: (0, 0)
    return pl.pallas_call(
        _merge_kernel,
        grid=(b, l // tm),
        in_specs=[
            pl.BlockSpec((1, tm, rdim), lambda i, t: (i, t, 0)),
            pl.BlockSpec((1, tm, rdim), lambda i, t: (i, t, 0)),
            pl.BlockSpec((1, tm, rdim), lambda i, t: (i, t, 0)),
            pl.BlockSpec((1, tm, LANES), lambda i, t: (i, t, 2)),
            pl.BlockSpec((1, tm, at.shape[-1]), lambda i, t: (i, t, 0)),
            pl.BlockSpec((1, tm, 2 * d), lambda i, t: (i, t, 0)),
            pl.BlockSpec((1, tm, d), lambda i, t: (i, t, 0)),
            pl.BlockSpec((1, 6, d), lambda i, t: (i, 0, 0)),
            pl.BlockSpec((1, rdim), const),
            pl.BlockSpec((1, rdim), const),
            pl.BlockSpec(g2.shape, const),
            pl.BlockSpec(w_br_rwkv.shape, const),
            pl.BlockSpec(w_br_attn.shape, const),
            pl.BlockSpec(w_out.shape, const),
            pl.BlockSpec((1, d), const),
            pl.BlockSpec(router_wt.shape, const),
        ],
        out_specs=[
            pl.BlockSpec((1, tm, d), lambda i, t: (i, t, 0)),
            pl.BlockSpec((1, tm, d), lambda i, t: (i, t, 0)),
            pl.BlockSpec((1, ne, tm), lambda i, t: (i, 0, t)),
        ],
        out_shape=[
            jax.ShapeDtypeStruct((b, l, d), F32),
            jax.ShapeDtypeStruct((b, l, d), BF16),
            jax.ShapeDtypeStruct((b, ne, l), F32),
        ],
        compiler_params=_params(("parallel", "parallel")),
        name="merge",
    )(y_f, y_b, bonus, z_lora, at, gates, x, mod, lnx_w.reshape(1, rdim), lnx_b.reshape(1, rdim), g2,
      w_br_rwkv, w_br_attn, w_out, norm2_g.reshape(1, d), router_wt)


def _prefix_count(mask_f, tri_excl):
    rows, l = mask_f.shape
    running = jnp.zeros((rows, 1), F32)
    pieces = []
    for blk in range(l // LANES):
        m = mask_f[:, blk * LANES:(blk + 1) * LANES]
        pieces.append(_dot(m.astype(BF16), tri_excl) + running)
        running = running + jnp.sum(m, axis=1, keepdims=True)
    return jnp.concatenate(pieces, axis=1)


def _route_kernel(aff_ref, sel_ref, *, cap):
    aff = aff_ref[0]
    bits = lax.bitcast_convert_type(aff, jnp.int32)
    thr = jnp.zeros((aff.shape[0], 1), jnp.int32)
    for bit in range(30, -1, -1):
        cand = thr | (1 << bit)
        cnt = jnp.sum(jnp.where(bits >= cand, 1.0, 0.0), axis=1, keepdims=True)
        thr = jnp.where(cnt >= cap, cand, thr)
    r = lax.broadcasted_iota(jnp.int32, (LANES, LANES), 0)
    c = lax.broadcasted_iota(jnp.int32, (LANES, LANES), 1)
    tri_excl = jnp.where(r < c, 1.0, 0.0).astype(BF16)
    gt = jnp.where(bits > thr, 1.0, 0.0)
    eq = jnp.where(bits == thr, 1.0, 0.0)
    need = cap - jnp.sum(gt, axis=1, keepdims=True)
    eq_rank = _prefix_count(eq, tri_excl)
    chosen = jnp.maximum(gt, jnp.where(eq_rank < need, eq, 0.0))
    rank = _prefix_count(chosen, tri_excl)
    sel_ref[0] = jnp.where(chosen > 0.5, rank, -1.0).astype(jnp.int32)


def _route(aff_t, cap):
    b, ne, l = aff_t.shape
    return pl.pallas_call(
        functools.partial(_route_kernel, cap=cap),
        grid=(b,),
        in_specs=[pl.BlockSpec((1, ne, l), lambda i: (i, 0, 0))],
        out_specs=pl.BlockSpec((1, ne, l), lambda i: (i, 0, 0)),
        out_shape=jax.ShapeDtypeStruct((b, ne, l), jnp.int32),
        compiler_params=_params(("parallel",)),
        name="route",
    )(aff_t)


def _moe_kernel(sel_ref, aff_ref, h_ref, x1_ref, mod_ref, wg_ref, wu_ref, wd_ref, o_ref, *, cap, scatter_tile):
    e = pl.program_id(1)
    l = h_ref.shape[1]

    @pl.when(e == 0)
    def _():
        o_ref[...] = x1_ref[...]

    sel_row = sel_ref[0, pl.ds(e, 1), :]
    aff_row = aff_ref[0, pl.ds(e, 1), :]
    slot = lax.broadcasted_iota(jnp.int32, (cap, l), 0)
    chosen = slot == sel_row
    onehot = jnp.where(chosen, 1.0, 0.0).astype(BF16)
    xe = _dot(onehot, h_ref[0]).astype(BF16)
    hg = _dot(xe, wg_ref[0])
    hu = _dot(xe, wu_ref[0])
    act = (hg * _sigmoid(hg) * hu).astype(BF16)
    ye = _dot(act, wd_ref[0])
    val = jnp.sum(jnp.where(chosen, aff_row, 0.0), axis=1, keepdims=True)
    yw = (ye * val * mod_ref[0, 5:6, :]).astype(BF16)
    for lt in range(l // scatter_tile):
        sl = slice(lt * scatter_tile, (lt + 1) * scatter_tile)
        o_ref[0, sl, :] += _dot_tn(onehot[:, sl], yw)


def _moe(sel, aff_t, h2, x1, mod, wg, wu, wd, cap):
    b, l, d = h2.shape
    ne = sel.shape[1]
    ff = wg.shape[-1]
    scatter_tile = min(l, 512)
    per_sample = lambda i, e: (i, 0, 0)
    per_expert = lambda i, e: (e, 0, 0)
    return pl.pallas_call(
        functools.partial(_moe_kernel, cap=cap, scatter_tile=scatter_tile),
        grid=(b, ne),
        in_specs=[
            pl.BlockSpec((1, ne, l), per_sample),
            pl.BlockSpec((1, ne, l), per_sample),
            pl.BlockSpec((1, l, d), per_sample, pipeline_mode=pl.Buffered(1)),
            pl.BlockSpec((1, l, d), per_sample, pipeline_mode=pl.Buffered(1)),
            pl.BlockSpec((1, 6, d), per_sample),
            pl.BlockSpec((1, d, ff), per_expert),
            pl.BlockSpec((1, d, ff), per_expert),
            pl.BlockSpec((1, ff, d), per_expert),
        ],
        out_specs=pl.BlockSpec((1, l, d), per_sample),
        out_shape=jax.ShapeDtypeStruct((b, l, d), F32),
        compiler_params=_params(("parallel", "arbitrary")),
        name="moe",
    )(sel, aff_t, h2, x1, mod, wg, wu, wd)


def _pad_rows(w, row0, rows):
    return jnp.zeros((rows, w.shape[1]), w.dtype).at[row0:row0 + w.shape[0]].set(w)


def _layer(x, ctx, mod_lat, mod_ctx, p):
    b, l, d = x.shape
    lc = ctx.shape[1]
    rdim = p["rwkv_w0"].shape[-1]
    dl = p["rwkv_w2"].shape[-2]
    al = p["rwkv_a2"].shape[-2]
    gl = p["rwkv_g2"].shape[0]
    att_dim = p["w_br_attn"].shape[0]
    kv_dim = (p["w_in"].shape[1] - 3 * rdim - 2 * dl - 2 * al - gl - att_dim - 2 * d) // 2
    assert 2 * dl == LANES and 2 * al == LANES and gl == LANES

    w_in = p["w_in"].astype(BF16)
    o = 0
    w_rkv = w_in[:, o:o + 3 * rdim]; o += 3 * rdim
    w_lora = w_in[:, o:o + 2 * dl + 2 * al + gl]; o += 2 * dl + 2 * al + gl
    w_q = w_in[:, o:o + att_dim]; o += att_dim
    w_kv = w_in[:, o:o + 2 * kv_dim]; o += 2 * kv_dim
    w_gates = w_in[:, o:o + 2 * d]

    tm = min(l, 512)
    z_rkv, z_lora, q, kv, gates = _proj(x, mod_lat, p["norm1_g"], [w_rkv, w_lora, w_q, w_kv, w_gates],
                                        min(l, 256), True)
    tmc = min(lc, 256)
    zc_rkv, zc_lora, kv_c = _proj(ctx, mod_ctx, p["norm1_g"], [w_rkv, w_lora, w_kv], tmc, False)

    conv_w8 = _pad_rows(p["rwkv_conv"], 0, 8)
    zeros3 = jnp.zeros((3, rdim), F32)
    pvec = jnp.stack([
        jnp.concatenate([p["rwkv_w0"][dd][None], p["rwkv_a0"][dd][None], p["rwkv_kk"][dd][None],
                         p["rwkv_ka"][dd][None], p["rwkv_rk"][dd].reshape(1, rdim), zeros3], axis=0)
        for dd in range(2)])
    w2_pad = jnp.stack([_pad_rows(p["rwkv_w2"][dd], dd * dl, LANES) for dd in range(2)]).astype(BF16)
    a2_pad = jnp.stack([_pad_rows(p["rwkv_a2"][dd], dd * al, LANES) for dd in range(2)]).astype(BF16)
    n_stack = rdim // MXU_DIM
    s0 = jnp.zeros((b, 2, n_stack, MXU_DIM, MXU_DIM), F32)
    (s_ctx,) = _wkv(zc_rkv, zc_lora, conv_w8, pvec, w2_pad, a2_pad, s0, False)
    y_f, y_b, bonus, _ = _wkv(z_rkv, z_lora, conv_w8, pvec, w2_pad, a2_pad, s_ctx, True)

    cos, sin = _rope_tables(l)
    reps = LANES // HEAD_DIM
    q_g = jnp.tile(p["q_norm_g"], reps).reshape(1, LANES)
    k_g = jnp.tile(p["k_norm_g"], reps).reshape(1, LANES)
    at = _attn(q, kv, kv_c, cos, sin, q_g, k_g, p["attn_sink"])

    x1, h2, aff_t = _merge(y_f, y_b, bonus, z_lora, at, gates, x, mod_lat, p["lnx_w"], p["lnx_b"],
                           p["rwkv_g2"].astype(BF16), p["w_br_rwkv"].astype(BF16),
                           p["w_br_attn"].astype(BF16), p["w_out"].astype(BF16), p["norm2_g"],
                           p["router_w"].T, tm)
    ne = p["router_w"].shape[1]
    cap = EC_FACTOR * l // ne
    sel = _route(aff_t, cap)
    return _moe(sel, aff_t, h2, x1, mod_lat, p["exp_w_gate"].astype(BF16), p["exp_w_up"].astype(BF16),
                p["exp_w_down"].astype(BF16), cap)


def kernel(x, c, ctx, c_ctx, ada_w, ada_b, norm1_g, norm2_g, w_in, rwkv_conv, rwkv_w0, rwkv_w2, rwkv_a0, rwkv_a2, rwkv_kk, rwkv_ka, rwkv_rk, rwkv_g2, lnx_w, lnx_b, q_norm_g, k_norm_g, attn_sink, w_br_rwkv, w_br_attn, w_out, router_w, exp_w_gate, exp_w_up, exp_w_down):
    b, l, d = x.shape
    depth = ada_w.shape[0]
    assert depth == 1, "the context stream is only read, never advanced, for a single layer"
    rows = -(-(b + 1) // 8) * 8
    c_all = jnp.zeros((rows, d), F32).at[:b].set(c).at[b].set(c_ctx)
    names = ["ada_w", "ada_b", "norm1_g", "norm2_g", "w_in", "rwkv_conv", "rwkv_w0", "rwkv_w2", "rwkv_a0",
             "rwkv_a2", "rwkv_kk", "rwkv_ka", "rwkv_rk", "rwkv_g2", "lnx_w", "lnx_b", "q_norm_g", "k_norm_g",
             "attn_sink", "w_br_rwkv", "w_br_attn", "w_out", "router_w", "exp_w_gate", "exp_w_up", "exp_w_down"]
    vals = [ada_w, ada_b, norm1_g, norm2_g, w_in, rwkv_conv, rwkv_w0, rwkv_w2, rwkv_a0, rwkv_a2, rwkv_kk,
            rwkv_ka, rwkv_rk, rwkv_g2, lnx_w, lnx_b, q_norm_g, k_norm_g, attn_sink, w_br_rwkv, w_br_attn,
            w_out, router_w, exp_w_gate, exp_w_up, exp_w_down]
    p = {n: v[0] for n, v in zip(names, vals)}
    mod = _adaln(c_all, p["ada_w"], p["ada_b"])
    mod_lat = mod[:b].reshape(b, 6, d)
    mod_ctx = mod[b:b + 1].reshape(1, 6, d)
    return _layer(x, ctx, mod_lat, mod_ctx, p)
```

```python
import functools

import jax
import jax.numpy as jnp
from jax import lax
from jax.experimental import pallas as pl
from jax.experimental.pallas import tpu as pltpu

F32 = jnp.float32
BF16 = jnp.bfloat16
HIGHEST = lax.Precision.HIGHEST

HEAD_DIM = 64
HEAD_SHIFT = 6
GRID_W = 64
WINDOW = 128
ATT_BLOCK = 128
ATT_Q_BLOCKS_PER_STEP = 4
ROPE_BASE = 10000.0
NORM_EPS = 1e-6
LNX_EPS = 64e-5
MASK_VALUE = -1e30
EC_FACTOR = 2

LANES = 128
MXU_DIM = 256
VMEM_LIMIT_BYTES = 56 * 1024 * 1024

WKV_CHUNK = 64
WKV_SAMPLES_PER_STEP = 4
WKV_CHUNKS_PER_STEP = 2
HEADS_PER_STACK = MXU_DIM // HEAD_DIM


def _dot(a, b):
    return jnp.dot(a, b, preferred_element_type=F32)


def _dot_nt(a, b):
    return lax.dot_general(a, b, (((1,), (1,)), ((), ())), preferred_element_type=F32)


def _dot_tn(a, b):
    return lax.dot_general(a, b, (((0,), (0,)), ((), ())), preferred_element_type=F32)


def _sigmoid(x):
    return 1.0 / (1.0 + jnp.exp(-x))


def _split_bf16(x, parts):
    out = []
    rem = x
    for _ in range(parts):
        hi = rem.astype(BF16)
        out.append(hi)
        rem = rem - hi.astype(F32)
    return out


def _dot_split_rhs(a_bf16, x, parts):
    acc = None
    for term in _split_bf16(x, parts):
        d = _dot(a_bf16, term)
        acc = d if acc is None else acc + d
    return acc


def _dot_split_lhs(x, b_bf16, parts):
    acc = None
    for term in _split_bf16(x, parts):
        d = _dot(term, b_bf16)
        acc = d if acc is None else acc + d
    return acc


def _head_ones(width):
    r = lax.broadcasted_iota(jnp.int32, (width, width), 0) >> HEAD_SHIFT
    c = lax.broadcasted_iota(jnp.int32, (width, width), 1) >> HEAD_SHIFT
    return jnp.where(r == c, 1.0, 0.0).astype(BF16)


def _head_sum(x, ones_bd):
    return _dot_split_lhs(x, ones_bd, 2)


def _params(semantics):
    return pltpu.CompilerParams(dimension_semantics=semantics, vmem_limit_bytes=VMEM_LIMIT_BYTES)


def _adaln_kernel(c_ref, w_ref, b_ref, o_ref):
    c = c_ref[...]
    s = c * _sigmoid(c)
    o_ref[...] = jnp.dot(s, w_ref[...], precision=HIGHEST, preferred_element_type=F32) + b_ref[...]


def _adaln(c_all, ada_w, ada_b):
    rows, d = c_all.shape
    n = ada_w.shape[1]
    tn = 512
    return pl.pallas_call(
        _adaln_kernel,
        grid=(n // tn,),
        in_specs=[
            pl.BlockSpec((rows, d), lambda j: (0, 0)),
            pl.BlockSpec((d, tn), lambda j: (0, j)),
            pl.BlockSpec((1, tn), lambda j: (0, j)),
        ],
        out_specs=pl.BlockSpec((rows, tn), lambda j: (0, j)),
        out_shape=jax.ShapeDtypeStruct((rows, n), F32),
        compiler_params=_params(("parallel",)),
        name="adaln",
    )(c_all, ada_w, ada_b.reshape(1, n))


def _proj_kernel(x_ref, mod_ref, g_ref, *refs):
    n_out = len(refs) // 2
    w_refs, o_refs = refs[:n_out], refs[n_out:]
    x = x_ref[0]
    ms = jnp.mean(x * x, axis=-1, keepdims=True)
    y = x * lax.rsqrt(ms + NORM_EPS) * g_ref[...]
    shift = mod_ref[0, 0:1, :]
    scale = mod_ref[0, 1:2, :]
    h = (y * (1.0 + scale) + shift).astype(BF16)
    for w_ref, o_ref in zip(w_refs, o_refs):
        o_ref[0] = _dot(h, w_ref[...])


def _proj(x, mod, norm_g, weights, tm, per_sample_mod):
    b, l, d = x.shape
    mod_map = (lambda i, t: (i, 0, 0)) if per_sample_mod else (lambda i, t: (0, 0, 0))
    in_specs = [
        pl.BlockSpec((1, tm, d), lambda i, t: (i, t, 0)),
        pl.BlockSpec((1, 6, d), mod_map),
        pl.BlockSpec((1, d), lambda i, t: (0, 0)),
    ]
    out_specs, out_shapes = [], []
    for w in weights:
        n = w.shape[1]
        in_specs.append(pl.BlockSpec((d, n), lambda i, t: (0, 0)))
        out_specs.append(pl.BlockSpec((1, tm, n), lambda i, t: (i, t, 0)))
        out_shapes.append(jax.ShapeDtypeStruct((b, l, n), F32))
    return pl.pallas_call(
        _proj_kernel,
        grid=(b, l // tm),
        in_specs=in_specs,
        out_specs=out_specs,
        out_shape=out_shapes,
        compiler_params=_params(("parallel", "parallel")),
        name="proj",
    )(x, mod, norm_g.reshape(1, d), *weights)


def _wkv_kernel(*refs, n_chunks, emit_y):
    t = WKV_CHUNK
    (zcf, zpf, znf, zlf, zcb, zpb, znb, zlb, cw_ref, pv_ref, w2_ref, a2_ref, s0_ref) = refs[:13]
    if emit_y:
        yf_ref, yb_ref, bo_ref, s1_ref, s_ref = refs[13:]
    else:
        s1_ref, s_ref = refs[13:]
    rdim = pv_ref.shape[-1]
    n_stack = rdim // MXU_DIM
    n_samples = zcf.shape[0]
    cps = zcf.shape[1] // t
    j = pl.program_id(1)

    @pl.when(j == 0)
    def _():
        s_ref[...] = s0_ref[...]

    ones_bd = _head_ones(MXU_DIM)
    sr = lax.broadcasted_iota(jnp.int32, (MXU_DIM, MXU_DIM), 0)
    sc = lax.broadcasted_iota(jnp.int32, (MXU_DIM, MXU_DIM), 1)
    stack_mask = jnp.where((sr >> HEAD_SHIFT) == (sc >> HEAD_SHIFT), 1.0, 0.0).astype(BF16)
    eye_b = jnp.where(sr == sc, 1.0, 0.0).astype(BF16)
    tr = lax.broadcasted_iota(jnp.int32, (t, t), 0)
    tc = lax.broadcasted_iota(jnp.int32, (t, t), 1)
    row = lax.broadcasted_iota(jnp.int32, (t, MXU_DIM), 0)
    head_mask = jnp.where((sr >> HEAD_SHIFT) == (sc >> HEAD_SHIFT), 1.0, 0.0)

    def stack(x):
        xb = x.astype(BF16)
        return jnp.concatenate([xb] * HEADS_PER_STACK, axis=0) * stack_mask

    def prepare(cj, bi, chains):
        for dd in range(2):
            rev = dd == 1
            done = j * cps + cj
            chunk = (n_chunks - 1 - done) if rev else done
            half = (cps - 1 - cj) if rev else cj
            r0 = half * t
            rows = slice(r0, r0 + t)
            zc, zp, zn, zl = (zcb, zpb, znb, zlb) if rev else (zcf, zpf, znf, zlf)
            has_prev = chunk > 0
            has_next = chunk < n_chunks - 1

            def conv(c0):
                cs = slice(c0, c0 + MXU_DIM)
                zm = zc[bi, rows, cs]
                if half > 0:
                    prev_row = zc[bi, r0 - 1:r0, cs]
                else:
                    prev_row = jnp.where(has_prev, zp[bi, 7:8, cs], 0.0)
                if half < cps - 1:
                    next_row = zc[bi, r0 + t:r0 + t + 1, cs]
                else:
                    next_row = jnp.where(has_next, zn[bi, 0:1, cs], 0.0)
                z_up = jnp.where(row == 0, prev_row, pltpu.roll(zm, 1, 0))
                z_dn = jnp.where(row == t - 1, next_row, pltpu.roll(zm, t - 1, 0))
                return cw_ref[0:1, cs] * z_up + cw_ref[1:2, cs] * zm + cw_ref[2:3, cs] * z_dn

            wd = zl[bi, rows, 0:LANES]
            ad = zl[bi, rows, LANES:2 * LANES].astype(BF16)
            wl = pv_ref[dd, 0:1, :] + _dot(jnp.tanh(wd).astype(BF16), w2_ref[dd])
            yield
            neg = -wl
            softplus = jnp.maximum(neg, 0.0) + jnp.log(1.0 + jnp.exp(-jnp.abs(neg)))
            lw = -jnp.exp(-softplus - 0.5)
            yield
            asig = _sigmoid(pv_ref[dd, 1:2, :] + _dot(ad, a2_ref[dd]))
            yield
            if emit_y and not rev:
                asig_o = _sigmoid(pv_ref[1, 1:2, :] + _dot(ad, a2_ref[1]))
                yield
            tri = jnp.where((tr <= tc) if rev else (tr >= tc), 1.0, 0.0).astype(BF16)
            cum = _dot_split_rhs(tri, lw, 3)
            yield
            c_end = cum[0:1, :] if rev else cum[t - 1:t, :]
            rho = 0.5 * c_end
            e_in = jnp.exp(cum - rho)
            yield
            e_out = jnp.exp(rho - cum)
            yield
            e_ex = jnp.exp(cum - lw - rho)
            e_rho = jnp.exp(rho)
            e_end = jnp.exp(c_end)
            yield
            strict = (sr < sc) if rev else (sr > sc)
            incl = (sr <= sc) if rev else (sr >= sc)

            for q in range(n_stack):
                sl = slice(q * MXU_DIM, (q + 1) * MXU_DIM)
                ch = dict(bi=bi, dd=dd, q=q, sl=sl, rows=rows, rev=rev, e_rho=e_rho[:, sl], e_end=e_end[:, sl])
                k_q = conv(rdim + q * MXU_DIM)
                yield
                v_q = conv(2 * rdim + q * MXU_DIM)
                yield
                kk_q = k_q * pv_ref[dd, 2:3, sl]
                ss = _head_sum(kk_q * kk_q, ones_bd)
                kkn = kk_q * lax.rsqrt(jnp.maximum(ss, 1e-24))
                yield
                k_mod = k_q * (1.0 + (asig[:, sl] - 1.0) * pv_ref[dd, 3:4, sl])
                a_u = (-kkn * e_ex[:, sl]).astype(BF16)
                b_u = (kkn * asig[:, sl] * e_out[:, sl]).astype(BF16)
                k_u = (k_mod * e_out[:, sl]).astype(BF16)
                v_u = v_q.astype(BF16)
                ch["v_u"] = v_u
                ch["bk_u"] = jnp.concatenate([b_u, k_u], axis=0)
                ch["a_st"] = stack(a_u)
                yield
                ch["b_st"] = stack(b_u)
                yield
                ch["k_st"] = stack(k_u)
                yield
                ch["v_st"] = stack(v_u)
                yield
                if emit_y:
                    r_q = conv(q * MXU_DIM)
                    yield
                    r_u = (r_q * e_in[:, sl]).astype(BF16)
                    ch["ar_u"] = jnp.concatenate([a_u, r_u], axis=0)
                    ch["ar_st"] = jnp.concatenate([ch["a_st"], stack(r_u)], axis=0)
                    yield
                    nb = _dot_nt(ch["ar_st"], ch["b_st"])
                    yield
                    nk = _dot_nt(ch["ar_st"], ch["k_st"])
                    yield
                    ch["a_rb"] = jnp.where(incl, nb[MXU_DIM:], 0.0).astype(BF16)
                    yield
                    ak = jnp.where(strict, nk[:MXU_DIM], 0.0).astype(BF16)
                    yield
                    ch["akrk"] = jnp.concatenate([ak, jnp.where(incl, nk[MXU_DIM:], 0.0).astype(BF16)], axis=0)
                    yield
                    if not rev:
                        k_mod_o = k_q * (1.0 + (asig_o[:, sl] - 1.0) * pv_ref[1, 3:4, sl])
                        rkk = r_q * (k_mod * pv_ref[0, 4:5, sl] + k_mod_o * pv_ref[1, 4:5, sl])
                        bo_ref[bi, rows, sl] = _head_sum(rkk, ones_bd) * v_q
                        yield
                else:
                    ch["ar_u"] = a_u
                    nb = _dot_nt(ch["a_st"], ch["b_st"])
                    yield
                    ch["akrk"] = jnp.where(strict, _dot_nt(ch["a_st"], ch["k_st"]), 0.0).astype(BF16)
                    yield
                ch["p"] = jnp.where(strict, nb[:MXU_DIM], 0.0).astype(BF16)
                ch["inv"] = ch["p"] + eye_b
                chains.append(ch)
                yield

    def advance(chains, filler):
        def tick():
            if filler is not None:
                next(filler, None)

        for _ in range(t.bit_length() - 2):
            for ch in chains:
                ch["p"] = _dot(ch["p"], ch["p"]).astype(BF16)
                tick()
            for ch in chains:
                ch["inv"] = _dot(ch["inv"], ch["p"] + eye_b).astype(BF16)
                tick()
        def unstack(m):
            out = m[0:t]
            for hh in range(1, HEADS_PER_STACK):
                out = out + m[hh * t:(hh + 1) * t]
            return out

        for ch in chains:
            ch["s_q"] = s_ref[ch["bi"], ch["dd"], ch["q"]]
            ch["s_rho"] = (ch["s_q"] * ch["e_rho"]).astype(BF16)
            ch["xs"] = _dot_nt(ch["ar_u"], ch["s_rho"])
            tick()
            ch["xv"] = _dot(ch["akrk"], ch["v_st"])
            tick()
            xs_st = jnp.concatenate([ch["xs"][0:t]] * HEADS_PER_STACK, axis=0) * head_mask
            ch["x"] = (xs_st + ch["xv"][:MXU_DIM]).astype(BF16)
        for ch in chains:
            z32 = _dot(ch["inv"], ch["x"])
            tick()
            ch["z"] = z32.astype(BF16)
            ch["zv_u"] = jnp.concatenate([unstack(z32).astype(BF16), ch["v_u"]], axis=0)
        for ch in chains:
            upd = _dot_tn(ch["zv_u"], ch["bk_u"]) * head_mask
            tick()
            s_ref[ch["bi"], ch["dd"], ch["q"]] = ch["s_q"] * ch["e_end"] + upd * ch["e_rho"]
        if emit_y:
            for ch in chains:
                y_mat = ch["xv"][MXU_DIM:] + _dot(ch["a_rb"], ch["z"])
                tick()
                out_ref = yb_ref if ch["rev"] else yf_ref
                out_ref[ch["bi"], ch["rows"], ch["sl"]] = ch["xs"][t:2 * t] + unstack(y_mat)

    items = [(cj, bi) for cj in range(cps) for bi in range(n_samples)]
    chain_lists = [[] for _ in items]
    preps = [prepare(cj, bi, chain_lists[k]) for k, (cj, bi) in enumerate(items)]
    for _ in preps[0]:
        pass
    for k in range(len(items)):
        filler = preps[k + 1] if k + 1 < len(items) else None
        advance(chain_lists[k], filler)
        if filler is not None:
            for _ in filler:
                pass

    @pl.when(j == n_chunks // cps - 1)
    def _():
        s1_ref[...] = s_ref[...]


def _wkv(z_rkv, z_lora, conv_w8, pvec, w2_pad, a2_pad, s0, emit_y):
    b, l, c3 = z_rkv.shape
    rdim = c3 // 3
    t = WKV_CHUNK
    nc = l // t
    n_stack = rdim // MXU_DIM
    last8 = l // 8 - 1
    bs = WKV_SAMPLES_PER_STEP if b % WKV_SAMPLES_PER_STEP == 0 else 1
    cps = WKV_CHUNKS_PER_STEP if nc % WKV_CHUNKS_PER_STEP == 0 else 1
    steps = nc // cps
    tb = cps * t
    hb = tb // 8

    def chunk_specs(cidx):
        return [
            pl.BlockSpec((bs, tb, c3), lambda i, jj: (i, cidx(jj), 0)),
            pl.BlockSpec((bs, 8, c3), lambda i, jj: (i, jnp.maximum(cidx(jj) * hb - 1, 0), 0)),
            pl.BlockSpec((bs, 8, c3), lambda i, jj: (i, jnp.minimum((cidx(jj) + 1) * hb, last8), 0)),
            pl.BlockSpec((bs, tb, 2 * LANES), lambda i, jj: (i, cidx(jj), 0)),
        ]

    fwd = lambda jj: jj
    bwd = lambda jj: steps - 1 - jj
    state_spec = pl.BlockSpec((bs, 2, n_stack, MXU_DIM, MXU_DIM), lambda i, jj: (i, 0, 0, 0, 0))
    in_specs = chunk_specs(fwd) + chunk_specs(bwd) + [
        pl.BlockSpec((8, c3), lambda i, jj: (0, 0)),
        pl.BlockSpec((2, 8, rdim), lambda i, jj: (0, 0, 0)),
        pl.BlockSpec((2, LANES, rdim), lambda i, jj: (0, 0, 0)),
        pl.BlockSpec((2, LANES, rdim), lambda i, jj: (0, 0, 0)),
        state_spec,
    ]
    out_specs, out_shapes = [], []
    if emit_y:
        out_specs += [
            pl.BlockSpec((bs, tb, rdim), lambda i, jj: (i, fwd(jj), 0)),
            pl.BlockSpec((bs, tb, rdim), lambda i, jj: (i, bwd(jj), 0)),
            pl.BlockSpec((bs, tb, rdim), lambda i, jj: (i, fwd(jj), 0)),
        ]
        out_shapes += [jax.ShapeDtypeStruct((b, l, rdim), F32)] * 3
    out_specs.append(state_spec)
    out_shapes.append(jax.ShapeDtypeStruct((b, 2, n_stack, MXU_DIM, MXU_DIM), F32))
    return pl.pallas_call(
        functools.partial(_wkv_kernel, n_chunks=nc, emit_y=emit_y),
        grid=(b // bs, steps),
        in_specs=in_specs,
        out_specs=out_specs,
        out_shape=out_shapes,
        scratch_shapes=[pltpu.VMEM((bs, 2, n_stack, MXU_DIM, MXU_DIM), F32)],
        compiler_params=_params(("parallel", "arbitrary")),
        name="wkv",
    )(*([z_rkv, z_rkv, z_rkv, z_lora] * 2), conv_w8, pvec, w2_pad, a2_pad, s0)


def _rope(x, cos, sin):
    lane = lax.broadcasted_iota(jnp.int32, x.shape, 1)
    first = (lane & 31) < 16
    partner = jnp.where(first, pltpu.roll(x, LANES - 16, 1), pltpu.roll(x, 16, 1))
    return x * cos + partner * sin


def _head_rmsnorm(x, g, ones_bd):
    ss = _head_sum(x * x, ones_bd)
    return x * lax.rsqrt(ss * (1.0 / HEAD_DIM) + NORM_EPS) * g


def _attn_kernel(q_ref, kv_ref, kvc_ref, cos_ref, sin_ref, qg_ref, kg_ref, sink_ref, o_ref,
                 kp_ref, vp_ref, kc_ref, vc_ref, *, seq_len, n_q_heads):
    blk = ATT_BLOCK
    i = pl.program_id(1)
    ones_bd = _head_ones(LANES)
    kg = kg_ref[...]
    n_kv = kv_ref.shape[-1] // (2 * HEAD_DIM)
    kw = n_kv * HEAD_DIM
    group = n_q_heads // n_kv
    lc = kvc_ref.shape[1]

    @pl.when(i == 0)
    def _():
        zeros = jnp.zeros((blk, kw), BF16)
        kp_ref[0:blk, :] = zeros
        vp_ref[0:blk, :] = zeros
        kp_ref[blk + seq_len:2 * blk + seq_len, :] = zeros
        vp_ref[blk + seq_len:2 * blk + seq_len, :] = zeros
        kk = _head_rmsnorm(kv_ref[0, :, 0:kw], kg, ones_bd)
        kp_ref[blk:blk + seq_len, :] = _rope(kk, cos_ref[...], sin_ref[...]).astype(BF16)
        vp_ref[blk:blk + seq_len, :] = kv_ref[0, :, kw:2 * kw].astype(BF16)
        kc_ref[...] = _head_rmsnorm(kvc_ref[0, :, 0:kw], kg, ones_bd).astype(BF16)
        vc_ref[...] = kvc_ref[0, :, kw:2 * kw].astype(BF16)

    k_ctx = kc_ref[...]
    v_ctx = vc_ref[...]
    lane = lax.broadcasted_iota(jnp.int32, (blk, LANES), 1)
    low_half = lane < HEAD_DIM
    high_half = lane >= HEAD_DIM
    scale = HEAD_DIM ** -0.5
    qr = lax.broadcasted_iota(jnp.int32, (blk, 3 * blk), 0)
    kc_i = lax.broadcasted_iota(jnp.int32, (blk, 3 * blk), 1)
    n_sub = q_ref.shape[1] // blk

    heads = []
    for sb in range(n_sub):
        start = pl.multiple_of((i * n_sub + sb) * blk, blk)
        rows = slice(sb * blk, (sb + 1) * blk)
        cos = cos_ref[pl.ds(start, blk), :]
        sin = sin_ref[pl.ds(start, blk), :]
        k_win = kp_ref[pl.ds(start, 3 * blk), :]
        v_win = vp_ref[pl.ds(start, 3 * blk), :]

        q_heads = []
        for m in range(n_q_heads // 2):
            slab = q_ref[0, rows, m * LANES:(m + 1) * LANES]
            slab = _rope(_head_rmsnorm(slab, qg_ref[...], ones_bd), cos, sin) * scale
            q_heads.append(slab)

        lo = jnp.maximum(qr, blk - start)
        hi = jnp.minimum(qr + 2 * WINDOW, seq_len + blk - 1 - start)
        hidden = jnp.where(kc_i < lo, 1.0, jnp.where(kc_i > hi, 1.0, 0.0))
        hidden = jnp.concatenate([hidden] * group, axis=0)

        for h in range(n_kv):
            kv_low = (h % 2) == 0
            parts = []
            for g in range(group):
                a = h * group + g
                slab = q_heads[a // 2]
                if ((a % 2) == 0) != kv_low:
                    slab = pltpu.roll(slab, HEAD_DIM, 1)
                parts.append(jnp.where(low_half if kv_low else high_half, slab, 0.0))
            sl = slice((h // 2) * LANES, (h // 2 + 1) * LANES)
            heads.append(dict(
                h=h, rows=rows, kv_low=kv_low, hidden=hidden,
                qg=jnp.concatenate(parts, axis=0).astype(BF16),
                k_all=jnp.concatenate([k_win[:, sl], k_ctx[:, sl]], axis=0),
                v_all=jnp.concatenate([v_win[:, sl], v_ctx[:, sl]], axis=0),
                sink=jnp.concatenate([jnp.full((blk, 1), sink_ref[h * group + g], F32) for g in range(group)],
                                     axis=0)))
    for hd in heads:
        s_all = _dot_nt(hd["qg"], hd["k_all"])
        hd["s"] = jnp.concatenate(
            [jnp.where(hd["hidden"] > 0.5, MASK_VALUE, s_all[:, 0:3 * blk]), s_all[:, 3 * blk:]], axis=1)
    for hd in heads:
        hd["mx"] = jnp.maximum(jnp.max(hd["s"], axis=-1, keepdims=True), hd["sink"])
    for hd in heads:
        hd["p"] = jnp.exp(hd["s"] - hd["mx"])
    for hd in heads:
        hd["den"] = jnp.sum(hd["p"], axis=-1, keepdims=True) + jnp.exp(hd["sink"] - hd["mx"])
    for hd in heads:
        hd["o"] = _dot(hd["p"].astype(BF16), hd["v_all"]) * (1.0 / hd["den"])
    for sb in range(n_sub):
        out_heads = [None] * n_q_heads
        for hd in heads[sb * n_kv:(sb + 1) * n_kv]:
            for g in range(group):
                a = hd["h"] * group + g
                o_a = hd["o"][g * blk:(g + 1) * blk]
                if ((a % 2) == 0) != hd["kv_low"]:
                    o_a = pltpu.roll(o_a, HEAD_DIM, 1)
                out_heads[a] = o_a
        rows = slice(sb * blk, (sb + 1) * blk)
        for m in range(n_q_heads // 2):
            o_ref[0, rows, m * LANES:(m + 1) * LANES] = jnp.where(low_half, out_heads[2 * m], out_heads[2 * m + 1])


def _attn(q, kv, kv_ctx, cos, sin, q_g, k_g, sink):
    b, l, qd = q.shape
    lc = kv_ctx.shape[1]
    kvd = kv.shape[-1]
    kw = kvd // 2
    n_q_heads = qd // HEAD_DIM
    blk = ATT_BLOCK
    tq = ATT_Q_BLOCKS_PER_STEP * blk if l % (ATT_Q_BLOCKS_PER_STEP * blk) == 0 else blk
    return pl.pallas_call(
        functools.partial(_attn_kernel, seq_len=l, n_q_heads=n_q_heads),
        grid=(b, l // tq),
        in_specs=[
            pl.BlockSpec((1, tq, qd), lambda i, t: (i, t, 0)),
            pl.BlockSpec((1, l, kvd), lambda i, t: (i, 0, 0)),
            pl.BlockSpec((1, lc, kvd), lambda i, t: (i, 0, 0)),
            pl.BlockSpec((l, LANES), lambda i, t: (0, 0)),
            pl.BlockSpec((l, LANES), lambda i, t: (0, 0)),
            pl.BlockSpec((1, LANES), lambda i, t: (0, 0)),
            pl.BlockSpec((1, LANES), lambda i, t: (0, 0)),
            pl.BlockSpec(memory_space=pltpu.SMEM),
        ],
        out_specs=pl.BlockSpec((1, tq, qd), lambda i, t: (i, t, 0)),
        out_shape=jax.ShapeDtypeStruct((b, l, qd), F32),
        scratch_shapes=[
            pltpu.VMEM((l + 2 * blk, kw), BF16),
            pltpu.VMEM((l + 2 * blk, kw), BF16),
            pltpu.VMEM((lc, kw), BF16),
            pltpu.VMEM((lc, kw), BF16),
        ],
        compiler_params=_params(("parallel", "arbitrary")),
        name="attn",
    )(q, kv, kv_ctx, cos, sin, q_g, k_g, sink)


def _rope_tables(seq_len):
    tpos = jnp.arange(seq_len, dtype=jnp.int32)
    row = (tpos // GRID_W).astype(F32)
    col = (tpos % GRID_W).astype(F32)
    dim = jnp.arange(HEAD_DIM, dtype=jnp.int32)
    half = HEAD_DIM // 4
    inv_freq = ROPE_BASE ** (-(dim % half).astype(F32) / half)
    pos = jnp.where((dim // (HEAD_DIM // 2))[None, :] == 0, row[:, None], col[:, None])
    ang = pos * inv_freq[None, :]
    sign = jnp.where((dim % (HEAD_DIM // 2)) < half, -1.0, 1.0)[None, :]
    cos = jnp.cos(ang)
    sin = jnp.sin(ang) * sign
    reps = LANES // HEAD_DIM
    return jnp.tile(cos, (1, reps)), jnp.tile(sin, (1, reps))


def _merge_kernel(yf_ref, yb_ref, bo_ref, gd_ref, at_ref, gates_ref, x_ref, mod_ref, lnw_ref, lnb_ref, g2_ref,
                  wbr_ref, wba_ref, wo_ref, n2_ref, rw_ref, x1_ref, h2_ref, aff_ref):
    d_model = x_ref.shape[-1]
    ones_bd = _head_ones(MXU_DIM)
    rdim = yf_ref.shape[-1]
    ys = yf_ref[0] + yb_ref[0]
    bonus = bo_ref[0]
    cols = []
    for q in range(rdim // MXU_DIM):
        sl = slice(q * MXU_DIM, (q + 1) * MXU_DIM)
        yq = ys[:, sl]
        mean = _head_sum(yq, ones_bd) * (1.0 / HEAD_DIM)
        diff = yq - mean
        var = _head_sum(diff * diff, ones_bd) * (1.0 / HEAD_DIM)
        cols.append(diff * lax.rsqrt(var + LNX_EPS))
    yn = jnp.concatenate(cols, axis=1)
    y = yn * lnw_ref[...] + lnb_ref[...] + bonus
    g = _dot(_sigmoid(gd_ref[0]).astype(BF16), g2_ref[...])
    o_rwkv = (y * g).astype(BF16)
    br = _dot(o_rwkv, wbr_ref[...])
    ba = _dot(at_ref[0].astype(BF16), wba_ref[...])
    gates = gates_ref[0]
    merged = _sigmoid(gates[:, 0:d_model]) * br + _sigmoid(gates[:, d_model:2 * d_model]) * ba
    out = _dot(merged.astype(BF16), wo_ref[...])
    x1 = x_ref[0] + mod_ref[0, 2:3, :] * out
    x1_ref[0] = x1
    ms = jnp.mean(x1 * x1, axis=-1, keepdims=True)
    h2 = x1 * lax.rsqrt(ms + NORM_EPS) * n2_ref[...]
    h2 = h2 * (1.0 + mod_ref[0, 4:5, :]) + mod_ref[0, 3:4, :]
    h_hi, h_lo = _split_bf16(h2, 2)
    h2_ref[0] = h_hi
    w_hi, w_lo = _split_bf16(rw_ref[...], 2)
    logits = _dot_nt(w_hi, h_hi) + _dot_nt(w_lo, h_hi) + _dot_nt(w_hi, h_lo)
    mx = jnp.max(logits, axis=0, keepdims=True)
    ex = jnp.exp(logits - mx)
    aff_ref[0] = ex / jnp.sum(ex, axis=0, keepdims=True)


def _merge(y_f, y_b, bonus, z_lora, at, gates, x, mod, lnx_w, lnx_b, g2, w_br_rwkv, w_br_attn, w_out,
           norm2_g, router_wt, tm):
    b, l, d = x.shape
    rdim = y_f.shape[-1]
    ne = router_wt.shape[0]
    const = lambda i, t: (0, 0)
    return pl.pallas_call(
        _merge_kernel,
        grid=(b, l // tm),
        in_specs=[
            pl.BlockSpec((1, tm, rdim), lambda i, t: (i, t, 0)),
            pl.BlockSpec((1, tm, rdim), lambda i, t: (i, t, 0)),
            pl.BlockSpec((1, tm, rdim), lambda i, t: (i, t, 0)),
            pl.BlockSpec((1, tm, LANES), lambda i, t: (i, t, 2)),
            pl.BlockSpec((1, tm, at.shape[-1]), lambda i, t: (i, t, 0)),
            pl.BlockSpec((1, tm, 2 * d), lambda i, t: (i, t, 0)),
            pl.BlockSpec((1, tm, d), lambda i, t: (i, t, 0)),
            pl.BlockSpec((1, 6, d), lambda i, t: (i, 0, 0)),
            pl.BlockSpec((1, rdim), const),
            pl.BlockSpec((1, rdim), const),
            pl.BlockSpec(g2.shape, const),
            pl.BlockSpec(w_br_rwkv.shape, const),
            pl.BlockSpec(w_br_attn.shape, const),
            pl.BlockSpec(w_out.shape, const),
            pl.BlockSpec((1, d), const),
            pl.BlockSpec(router_wt.shape, const),
        ],
        out_specs=[
            pl.BlockSpec((1, tm, d), lambda i, t: (i, t, 0)),
            pl.BlockSpec((1, tm, d), lambda i, t: (i, t, 0)),
            pl.BlockSpec((1, ne, tm), lambda i, t: (i, 0, t)),
        ],
        out_shape=[
            jax.ShapeDtypeStruct((b, l, d), F32),
            jax.ShapeDtypeStruct((b, l, d), BF16),
            jax.ShapeDtypeStruct((b, ne, l), F32),
        ],
        compiler_params=_params(("parallel", "parallel")),
        name="merge",
    )(y_f, y_b, bonus, z_lora, at, gates, x, mod, lnx_w.reshape(1, rdim), lnx_b.reshape(1, rdim), g2,
      w_br_rwkv, w_br_attn, w_out, norm2_g.reshape(1, d), router_wt)


def _prefix_count(mask_f, tri_excl):
    rows, l = mask_f.shape
    running = jnp.zeros((rows, 1), F32)
    pieces = []
    for blk in range(l // LANES):
        m = mask_f[:, blk * LANES:(blk + 1) * LANES]
        pieces.append(_dot(m.astype(BF16), tri_excl) + running)
        running = running + jnp.sum(m, axis=1, keepdims=True)
    return jnp.concatenate(pieces, axis=1)


def _route_kernel(aff_ref, sel_ref, *, cap):
    aff = aff_ref[0]
    bits = lax.bitcast_convert_type(aff, jnp.int32)
    thr = jnp.zeros((aff.shape[0], 1), jnp.int32)
    for bit in range(30, -1, -1):
        cand = thr | (1 << bit)
        cnt = jnp.sum(jnp.where(bits >= cand, 1.0, 0.0), axis=1, keepdims=True)
        thr = jnp.where(cnt >= cap, cand, thr)
    r = lax.broadcasted_iota(jnp.int32, (LANES, LANES), 0)
    c = lax.broadcasted_iota(jnp.int32, (LANES, LANES), 1)
    tri_excl = jnp.where(r < c, 1.0, 0.0).astype(BF16)
    gt = jnp.where(bits > thr, 1.0, 0.0)
    eq = jnp.where(bits == thr, 1.0, 0.0)
    need = cap - jnp.sum(gt, axis=1, keepdims=True)
    eq_rank = _prefix_count(eq, tri_excl)
    chosen = jnp.maximum(gt, jnp.where(eq_rank < need, eq, 0.0))
    rank = _prefix_count(chosen, tri_excl)
    sel_ref[0] = jnp.where(chosen > 0.5, rank, -1.0).astype(jnp.int32)


def _route(aff_t, cap):
    b, ne, l = aff_t.shape
    return pl.pallas_call(
        functools.partial(_route_kernel, cap=cap),
        grid=(b,),
        in_specs=[pl.BlockSpec((1, ne, l), lambda i: (i, 0, 0))],
        out_specs=pl.BlockSpec((1, ne, l), lambda i: (i, 0, 0)),
        out_shape=jax.ShapeDtypeStruct((b, ne, l), jnp.int32),
        compiler_params=_params(("parallel",)),
        name="route",
    )(aff_t)


def _moe_kernel(sel_ref, aff_ref, h_ref, x1_ref, mod_ref, wg_ref, wu_ref, wd_ref, o_ref, *, cap, scatter_tile):
    e = pl.program_id(1)
    l = h_ref.shape[1]

    @pl.when(e == 0)
    def _():
        o_ref[...] = x1_ref[...]

    sel_row = sel_ref[0, pl.ds(e, 1), :]
    aff_row = aff_ref[0, pl.ds(e, 1), :]
    slot = lax.broadcasted_iota(jnp.int32, (cap, l), 0)
    chosen = slot == sel_row
    onehot = jnp.where(chosen, 1.0, 0.0).astype(BF16)
    xe = _dot(onehot, h_ref[0]).astype(BF16)
    hg = _dot(xe, wg_ref[0])
    hu = _dot(xe, wu_ref[0])
    act = (hg * _sigmoid(hg) * hu).astype(BF16)
    ye = _dot(act, wd_ref[0])
    val = jnp.sum(jnp.where(chosen, aff_row, 0.0), axis=1, keepdims=True)
    yw = (ye * val * mod_ref[0, 5:6, :]).astype(BF16)
    for lt in range(l // scatter_tile):
        sl = slice(lt * scatter_tile, (lt + 1) * scatter_tile)
        o_ref[0, sl, :] += _dot_tn(onehot[:, sl], yw)


def _moe(sel, aff_t, h2, x1, mod, wg, wu, wd, cap):
    b, l, d = h2.shape
    ne = sel.shape[1]
    ff = wg.shape[-1]
    scatter_tile = min(l, 512)
    per_sample = lambda i, e: (i, 0, 0)
    per_expert = lambda i, e: (e, 0, 0)
    return pl.pallas_call(
        functools.partial(_moe_kernel, cap=cap, scatter_tile=scatter_tile),
        grid=(b, ne),
        in_specs=[
            pl.BlockSpec((1, ne, l), per_sample),
            pl.BlockSpec((1, ne, l), per_sample),
            pl.BlockSpec((1, l, d), per_sample, pipeline_mode=pl.Buffered(1)),
            pl.BlockSpec((1, l, d), per_sample, pipeline_mode=pl.Buffered(1)),
            pl.BlockSpec((1, 6, d), per_sample),
            pl.BlockSpec((1, d, ff), per_expert),
            pl.BlockSpec((1, d, ff), per_expert),
            pl.BlockSpec((1, ff, d), per_expert),
        ],
        out_specs=pl.BlockSpec((1, l, d), per_sample),
        out_shape=jax.ShapeDtypeStruct((b, l, d), F32),
        compiler_params=_params(("parallel", "arbitrary")),
        name="moe",
    )(sel, aff_t, h2, x1, mod, wg, wu, wd)


def _pad_rows(w, row0, rows):
    return jnp.zeros((rows, w.shape[1]), w.dtype).at[row0:row0 + w.shape[0]].set(w)


def _layer(x, ctx, mod_lat, mod_ctx, p):
    b, l, d = x.shape
    lc = ctx.shape[1]
    rdim = p["rwkv_w0"].shape[-1]
    dl = p["rwkv_w2"].shape[-2]
    al = p["rwkv_a2"].shape[-2]
    gl = p["rwkv_g2"].shape[0]
    att_dim = p["w_br_attn"].shape[0]
    kv_dim = (p["w_in"].shape[1] - 3 * rdim - 2 * dl - 2 * al - gl - att_dim - 2 * d) // 2
    assert 2 * dl == LANES and 2 * al == LANES and gl == LANES

    w_in = p["w_in"].astype(BF16)
    o = 0
    w_rkv = w_in[:, o:o + 3 * rdim]; o += 3 * rdim
    w_lora = w_in[:, o:o + 2 * dl + 2 * al + gl]; o += 2 * dl + 2 * al + gl
    w_q = w_in[:, o:o + att_dim]; o += att_dim
    w_kv = w_in[:, o:o + 2 * kv_dim]; o += 2 * kv_dim
    w_gates = w_in[:, o:o + 2 * d]

    tm = min(l, 512)
    z_rkv, z_lora, q, kv, gates = _proj(x, mod_lat, p["norm1_g"], [w_rkv, w_lora, w_q, w_kv, w_gates],
                                        min(l, 256), True)
    tmc = min(lc, 256)
    zc_rkv, zc_lora, kv_c = _proj(ctx, mod_ctx, p["norm1_g"], [w_rkv, w_lora, w_kv], tmc, False)

    conv_w8 = _pad_rows(p["rwkv_conv"], 0, 8)
    zeros3 = jnp.zeros((3, rdim), F32)
    pvec = jnp.stack([
        jnp.concatenate([p["rwkv_w0"][dd][None], p["rwkv_a0"][dd][None], p["rwkv_kk"][dd][None],
                         p["rwkv_ka"][dd][None], p["rwkv_rk"][dd].reshape(1, rdim), zeros3], axis=0)
        for dd in range(2)])
    w2_pad = jnp.stack([_pad_rows(p["rwkv_w2"][dd], dd * dl, LANES) for dd in range(2)]).astype(BF16)
    a2_pad = jnp.stack([_pad_rows(p["rwkv_a2"][dd], dd * al, LANES) for dd in range(2)]).astype(BF16)
    n_stack = rdim // MXU_DIM
    s0 = jnp.zeros((b, 2, n_stack, MXU_DIM, MXU_DIM), F32)
    (s_ctx,) = _wkv(zc_rkv, zc_lora, conv_w8, pvec, w2_pad, a2_pad, s0, False)
    y_f, y_b, bonus, _ = _wkv(z_rkv, z_lora, conv_w8, pvec, w2_pad, a2_pad, s_ctx, True)

    cos, sin = _rope_tables(l)
    reps = LANES // HEAD_DIM
    q_g = jnp.tile(p["q_norm_g"], reps).reshape(1, LANES)
    k_g = jnp.tile(p["k_norm_g"], reps).reshape(1, LANES)
    at = _attn(q, kv, kv_c, cos, sin, q_g, k_g, p["attn_sink"])

    x1, h2, aff_t = _merge(y_f, y_b, bonus, z_lora, at, gates, x, mod_lat, p["lnx_w"], p["lnx_b"],
                           p["rwkv_g2"].astype(BF16), p["w_br_rwkv"].astype(BF16),
                           p["w_br_attn"].astype(BF16), p["w_out"].astype(BF16), p["norm2_g"],
                           p["router_w"].T, tm)
    ne = p["router_w"].shape[1]
    cap = EC_FACTOR * l // ne
    sel = _route(aff_t, cap)
    return _moe(sel, aff_t, h2, x1, mod_lat, p["exp_w_gate"].astype(BF16), p["exp_w_up"].astype(BF16),
                p["exp_w_down"].astype(BF16), cap)


def kernel(x, c, ctx, c_ctx, ada_w, ada_b, norm1_g, norm2_g, w_in, rwkv_conv, rwkv_w0, rwkv_w2, rwkv_a0, rwkv_a2, rwkv_kk, rwkv_ka, rwkv_rk, rwkv_g2, lnx_w, lnx_b, q_norm_g, k_norm_g, attn_sink, w_br_rwkv, w_br_attn, w_out, router_w, exp_w_gate, exp_w_up, exp_w_down):
    b, l, d = x.shape
    depth = ada_w.shape[0]
    assert depth == 1, "the context stream is only read, never advanced, for a single layer"
    rows = -(-(b + 1) // 8) * 8
    c_all = jnp.zeros((rows, d), F32).at[:b].set(c).at[b].set(c_ctx)
    names = ["ada_w", "ada_b", "norm1_g", "norm2_g", "w_in", "rwkv_conv", "rwkv_w0", "rwkv_w2", "rwkv_a0",
             "rwkv_a2", "rwkv_kk", "rwkv_ka", "rwkv_rk", "rwkv_g2", "lnx_w", "lnx_b", "q_norm_g", "k_norm_g",
             "attn_sink", "w_br_rwkv", "w_br_attn", "w_out", "router_w", "exp_w_gate", "exp_w_up", "exp_w_down"]
    vals = [ada_w, ada_b, norm1_g, norm2_g, w_in, rwkv_conv, rwkv_w0, rwkv_w2, rwkv_a0, rwkv_a2, rwkv_kk,
            rwkv_ka, rwkv_rk, rwkv_g2, lnx_w, lnx_b, q_norm_g, k_norm_g, attn_sink, w_br_rwkv, w_br_attn,
            w_out, router_w, exp_w_gate, exp_w_up, exp_w_down]
    p = {n: v[0] for n, v in zip(names, vals)}
    mod = _adaln(c_all, p["ada_w"], p["ada_b"])
    mod_lat = mod[:b].reshape(b, 6, d)
    mod_ctx = mod[b:b + 1].reshape(1, 6, d)
    return _layer(x, ctx, mod_lat, mod_ctx, p)
```

```python
import functools

import jax
import jax.numpy as jnp
from jax import lax
from jax.experimental import pallas as pl
from jax.experimental.pallas import tpu as pltpu

F32 = jnp.float32
BF16 = jnp.bfloat16
HIGHEST = lax.Precision.HIGHEST

HEAD_DIM = 64
HEAD_SHIFT = 6
GRID_W = 64
WINDOW = 128
ATT_BLOCK = 128
ATT_Q_BLOCKS_PER_STEP = 4
ROPE_BASE = 10000.0
NORM_EPS = 1e-6
LNX_EPS = 64e-5
MASK_VALUE = -1e30
EC_FACTOR = 2

LANES = 128
MXU_DIM = 256
VMEM_LIMIT_BYTES = 56 * 1024 * 1024

WKV_CHUNK = 64
WKV_SAMPLES_PER_STEP = 4
WKV_CHUNKS_PER_STEP = 2
HEADS_PER_STACK = MXU_DIM // HEAD_DIM


def _dot(a, b):
    return jnp.dot(a, b, preferred_element_type=F32)


def _dot_nt(a, b):
    return lax.dot_general(a, b, (((1,), (1,)), ((), ())), preferred_element_type=F32)


def _dot_tn(a, b):
    return lax.dot_general(a, b, (((0,), (0,)), ((), ())), preferred_element_type=F32)


def _sigmoid(x):
    return 1.0 / (1.0 + jnp.exp(-x))


def _split_bf16(x, parts):
    out = []
    rem = x
    for _ in range(parts):
        hi = rem.astype(BF16)
        out.append(hi)
        rem = rem - hi.astype(F32)
    return out


def _dot_split_rhs(a_bf16, x, parts):
    acc = None
    for term in _split_bf16(x, parts):
        d = _dot(a_bf16, term)
        acc = d if acc is None else acc + d
    return acc


def _dot_split_lhs(x, b_bf16, parts):
    acc = None
    for term in _split_bf16(x, parts):
        d = _dot(term, b_bf16)
        acc = d if acc is None else acc + d
    return acc


def _head_ones(width):
    r = lax.broadcasted_iota(jnp.int32, (width, width), 0) >> HEAD_SHIFT
    c = lax.broadcasted_iota(jnp.int32, (width, width), 1) >> HEAD_SHIFT
    return jnp.where(r == c, 1.0, 0.0).astype(BF16)


def _head_sum(x, ones_bd):
    return _dot_split_lhs(x, ones_bd, 2)


def _params(semantics):
    return pltpu.CompilerParams(dimension_semantics=semantics, vmem_limit_bytes=VMEM_LIMIT_BYTES)


def _adaln_kernel(c_ref, w_ref, b_ref, o_ref):
    c = c_ref[...]
    s = c * _sigmoid(c)
    o_ref[...] = jnp.dot(s, w_ref[...], precision=HIGHEST, preferred_element_type=F32) + b_ref[...]


def _adaln(c_all, ada_w, ada_b):
    rows, d = c_all.shape
    n = ada_w.shape[1]
    tn = 512
    return pl.pallas_call(
        _adaln_kernel,
        grid=(n // tn,),
        in_specs=[
            pl.BlockSpec((rows, d), lambda j: (0, 0)),
            pl.BlockSpec((d, tn), lambda j: (0, j)),
            pl.BlockSpec((1, tn), lambda j: (0, j)),
        ],
        out_specs=pl.BlockSpec((rows, tn), lambda j: (0, j)),
        out_shape=jax.ShapeDtypeStruct((rows, n), F32),
        compiler_params=_params(("parallel",)),
        name="adaln",
    )(c_all, ada_w, ada_b.reshape(1, n))


def _proj_kernel(x_ref, mod_ref, g_ref, *refs):
    n_out = len(refs) // 2
    w_refs, o_refs = refs[:n_out], refs[n_out:]
    x = x_ref[0]
    ms = jnp.mean(x * x, axis=-1, keepdims=True)
    y = x * lax.rsqrt(ms + NORM_EPS) * g_ref[...]
    shift = mod_ref[0, 0:1, :]
    scale = mod_ref[0, 1:2, :]
    h = (y * (1.0 + scale) + shift).astype(BF16)
    for w_ref, o_ref in zip(w_refs, o_refs):
        o_ref[0] = _dot(h, w_ref[...])


def _proj(x, mod, norm_g, weights, tm, per_sample_mod):
    b, l, d = x.shape
    mod_map = (lambda i, t: (i, 0, 0)) if per_sample_mod else (lambda i, t: (0, 0, 0))
    in_specs = [
        pl.BlockSpec((1, tm, d), lambda i, t: (i, t, 0)),
        pl.BlockSpec((1, 6, d), mod_map),
        pl.BlockSpec((1, d), lambda i, t: (0, 0)),
    ]
    out_specs, out_shapes = [], []
    for w in weights:
        n = w.shape[1]
        in_specs.append(pl.BlockSpec((d, n), lambda i, t: (0, 0), pipeline_mode=pl.Buffered(1)))
        out_specs.append(pl.BlockSpec((1, tm, n), lambda i, t: (i, t, 0)))
        out_shapes.append(jax.ShapeDtypeStruct((b, l, n), F32))
    return pl.pallas_call(
        _proj_kernel,
        grid=(b, l // tm),
        in_specs=in_specs,
        out_specs=out_specs,
        out_shape=out_shapes,
        compiler_params=_params(("parallel", "parallel")),
        name="proj",
    )(x, mod, norm_g.reshape(1, d), *weights)


def _wkv_kernel(*refs, n_chunks, emit_y):
    t = WKV_CHUNK
    (zcf, zpf, znf, zlf, zcb, zpb, znb, zlb, cw_ref, pv_ref, w2_ref, a2_ref, s0_ref) = refs[:13]
    if emit_y:
        yf_ref, yb_ref, bo_ref, s1_ref, s_ref = refs[13:]
    else:
        s1_ref, s_ref = refs[13:]
    rdim = pv_ref.shape[-1]
    n_stack = rdim // MXU_DIM
    n_samples = zcf.shape[0]
    cps = zcf.shape[1] // t
    j = pl.program_id(1)

    @pl.when(j == 0)
    def _():
        s_ref[...] = s0_ref[...]

    ones_bd = _head_ones(MXU_DIM)
    sr = lax.broadcasted_iota(jnp.int32, (MXU_DIM, MXU_DIM), 0)
    sc = lax.broadcasted_iota(jnp.int32, (MXU_DIM, MXU_DIM), 1)
    stack_mask = jnp.where((sr >> HEAD_SHIFT) == (sc >> HEAD_SHIFT), 1.0, 0.0).astype(BF16)
    eye_b = jnp.where(sr == sc, 1.0, 0.0).astype(BF16)
    tr = lax.broadcasted_iota(jnp.int32, (t, t), 0)
    tc = lax.broadcasted_iota(jnp.int32, (t, t), 1)
    row = lax.broadcasted_iota(jnp.int32, (t, MXU_DIM), 0)
    head_mask = jnp.where((sr >> HEAD_SHIFT) == (sc >> HEAD_SHIFT), 1.0, 0.0)

    def stack(x):
        xb = x.astype(BF16)
        return jnp.concatenate([xb] * HEADS_PER_STACK, axis=0) * stack_mask

    def prepare(cj, bi, chains):
        for dd in range(2):
            rev = dd == 1
            done = j * cps + cj
            chunk = (n_chunks - 1 - done) if rev else done
            half = (cps - 1 - cj) if rev else cj
            r0 = half * t
            rows = slice(r0, r0 + t)
            zc, zp, zn, zl = (zcb, zpb, znb, zlb) if rev else (zcf, zpf, znf, zlf)
            has_prev = chunk > 0
            has_next = chunk < n_chunks - 1

            def conv(c0):
                cs = slice(c0, c0 + MXU_DIM)
                zm = zc[bi, rows, cs]
                if half > 0:
                    prev_row = zc[bi, r0 - 1:r0, cs]
                else:
                    prev_row = jnp.where(has_prev, zp[bi, 7:8, cs], 0.0)
                if half < cps - 1:
                    next_row = zc[bi, r0 + t:r0 + t + 1, cs]
                else:
                    next_row = jnp.where(has_next, zn[bi, 0:1, cs], 0.0)
                z_up = jnp.where(row == 0, prev_row, pltpu.roll(zm, 1, 0))
                z_dn = jnp.where(row == t - 1, next_row, pltpu.roll(zm, t - 1, 0))
                return cw_ref[0:1, cs] * z_up + cw_ref[1:2, cs] * zm + cw_ref[2:3, cs] * z_dn

            wd = zl[bi, rows, 0:LANES]
            ad = zl[bi, rows, LANES:2 * LANES].astype(BF16)
            wl = pv_ref[dd, 0:1, :] + _dot(jnp.tanh(wd).astype(BF16), w2_ref[dd])
            yield
            neg = -wl
            softplus = jnp.maximum(neg, 0.0) + jnp.log(1.0 + jnp.exp(-jnp.abs(neg)))
            lw = -jnp.exp(-softplus - 0.5)
            yield
            asig = _sigmoid(pv_ref[dd, 1:2, :] + _dot(ad, a2_ref[dd]))
            yield
            if emit_y and not rev:
                asig_o = _sigmoid(pv_ref[1, 1:2, :] + _dot(ad, a2_ref[1]))
                yield
            tri = jnp.where((tr <= tc) if rev else (tr >= tc), 1.0, 0.0).astype(BF16)
            cum = _dot_split_rhs(tri, lw, 3)
            yield
            c_end = cum[0:1, :] if rev else cum[t - 1:t, :]
            rho = 0.5 * c_end
            e_in = jnp.exp(cum - rho)
            yield
            e_out = jnp.exp(rho - cum)
            yield
            e_ex = jnp.exp(cum - lw - rho)
            e_rho = jnp.exp(rho)
            e_end = jnp.exp(c_end)
            yield
            strict = (sr < sc) if rev else (sr > sc)
            incl = (sr <= sc) if rev else (sr >= sc)

            for q in range(n_stack):
                sl = slice(q * MXU_DIM, (q + 1) * MXU_DIM)
                ch = dict(bi=bi, dd=dd, q=q, sl=sl, rows=rows, rev=rev, e_rho=e_rho[:, sl], e_end=e_end[:, sl])
                k_q = conv(rdim + q * MXU_DIM)
                yield
                v_q = conv(2 * rdim + q * MXU_DIM)
                yield
                kk_q = k_q * pv_ref[dd, 2:3, sl]
                ss = _head_sum(kk_q * kk_q, ones_bd)
                kkn = kk_q * lax.rsqrt(jnp.maximum(ss, 1e-24))
                yield
                k_mod = k_q * (1.0 + (asig[:, sl] - 1.0) * pv_ref[dd, 3:4, sl])
                a_u = (-kkn * e_ex[:, sl]).astype(BF16)
                b_u = (kkn * asig[:, sl] * e_out[:, sl]).astype(BF16)
                k_u = (k_mod * e_out[:, sl]).astype(BF16)
                v_u = v_q.astype(BF16)
                ch["v_u"] = v_u
                ch["bk_u"] = jnp.concatenate([b_u, k_u], axis=0)
                ch["a_st"] = stack(a_u)
                yield
                ch["b_st"] = stack(b_u)
                yield
                ch["k_st"] = stack(k_u)
                yield
                ch["v_st"] = stack(v_u)
                yield
                if emit_y:
                    r_q = conv(q * MXU_DIM)
                    yield
                    r_u = (r_q * e_in[:, sl]).astype(BF16)
                    ch["ar_u"] = jnp.concatenate([a_u, r_u], axis=0)
                    ch["ar_st"] = jnp.concatenate([ch["a_st"], stack(r_u)], axis=0)
                    yield
                    nb = _dot_nt(ch["ar_st"], ch["b_st"])
                    yield
                    nk = _dot_nt(ch["ar_st"], ch["k_st"])
                    yield
                    ch["a_rb"] = jnp.where(incl, nb[MXU_DIM:], 0.0).astype(BF16)
                    yield
                    ak = jnp.where(strict, nk[:MXU_DIM], 0.0).astype(BF16)
                    yield
                    ch["akrk"] = jnp.concatenate([ak, jnp.where(incl, nk[MXU_DIM:], 0.0).astype(BF16)], axis=0)
                    yield
                    if not rev:
                        k_mod_o = k_q * (1.0 + (asig_o[:, sl] - 1.0) * pv_ref[1, 3:4, sl])
                        rkk = r_q * (k_mod * pv_ref[0, 4:5, sl] + k_mod_o * pv_ref[1, 4:5, sl])
                        bo_ref[bi, rows, sl] = _head_sum(rkk, ones_bd) * v_q
                        yield
                else:
                    ch["ar_u"] = a_u
                    nb = _dot_nt(ch["a_st"], ch["b_st"])
                    yield
                    ch["akrk"] = jnp.where(strict, _dot_nt(ch["a_st"], ch["k_st"]), 0.0).astype(BF16)
                    yield
                ch["p"] = jnp.where(strict, nb[:MXU_DIM], 0.0).astype(BF16)
                ch["inv"] = ch["p"] + eye_b
                chains.append(ch)
                yield

    def advance(chains, filler):
        def tick():
            if filler is not None:
                next(filler, None)

        for _ in range(t.bit_length() - 2):
            for ch in chains:
                ch["p"] = _dot(ch["p"], ch["p"]).astype(BF16)
                tick()
            for ch in chains:
                ch["inv"] = _dot(ch["inv"], ch["p"] + eye_b).astype(BF16)
                tick()
        def unstack(m):
            out = m[0:t]
            for hh in range(1, HEADS_PER_STACK):
                out = out + m[hh * t:(hh + 1) * t]
            return out

        for ch in chains:
            ch["s_q"] = s_ref[ch["bi"], ch["dd"], ch["q"]]
            ch["s_rho"] = (ch["s_q"] * ch["e_rho"]).astype(BF16)
            ch["xs"] = _dot_nt(ch["ar_u"], ch["s_rho"])
            tick()
            ch["xv"] = _dot(ch["akrk"], ch["v_st"])
            tick()
            xs_st = jnp.concatenate([ch["xs"][0:t]] * HEADS_PER_STACK, axis=0) * head_mask
            ch["x"] = (xs_st + ch["xv"][:MXU_DIM]).astype(BF16)
        for ch in chains:
            z32 = _dot(ch["inv"], ch["x"])
            tick()
            ch["z"] = z32.astype(BF16)
            ch["zv_u"] = jnp.concatenate([unstack(z32).astype(BF16), ch["v_u"]], axis=0)
        for ch in chains:
            upd = _dot_tn(ch["zv_u"], ch["bk_u"]) * head_mask
            tick()
            s_ref[ch["bi"], ch["dd"], ch["q"]] = ch["s_q"] * ch["e_end"] + upd * ch["e_rho"]
        if emit_y:
            for ch in chains:
                y_mat = ch["xv"][MXU_DIM:] + _dot(ch["a_rb"], ch["z"])
                tick()
                out_ref = yb_ref if ch["rev"] else yf_ref
                out_ref[ch["bi"], ch["rows"], ch["sl"]] = ch["xs"][t:2 * t] + unstack(y_mat)

    items = [(cj, bi) for cj in range(cps) for bi in range(n_samples)]
    chain_lists = [[] for _ in items]
    preps = [prepare(cj, bi, chain_lists[k]) for k, (cj, bi) in enumerate(items)]
    for _ in preps[0]:
        pass
    for k in range(len(items)):
        filler = preps[k + 1] if k + 1 < len(items) else None
        advance(chain_lists[k], filler)
        if filler is not None:
            for _ in filler:
                pass

    @pl.when(j == n_chunks // cps - 1)
    def _():
        s1_ref[...] = s_ref[...]


def _wkv(z_rkv, z_lora, conv_w8, pvec, w2_pad, a2_pad, s0, emit_y):
    b, l, c3 = z_rkv.shape
    rdim = c3 // 3
    t = WKV_CHUNK
    nc = l // t
    n_stack = rdim // MXU_DIM
    last8 = l // 8 - 1
    bs = WKV_SAMPLES_PER_STEP if b % WKV_SAMPLES_PER_STEP == 0 else 1
    cps = WKV_CHUNKS_PER_STEP if nc % WKV_CHUNKS_PER_STEP == 0 else 1
    steps = nc // cps
    tb = cps * t
    hb = tb // 8

    def chunk_specs(cidx):
        return [
            pl.BlockSpec((bs, tb, c3), lambda i, jj: (i, cidx(jj), 0)),
            pl.BlockSpec((bs, 8, c3), lambda i, jj: (i, jnp.maximum(cidx(jj) * hb - 1, 0), 0)),
            pl.BlockSpec((bs, 8, c3), lambda i, jj: (i, jnp.minimum((cidx(jj) + 1) * hb, last8), 0)),
            pl.BlockSpec((bs, tb, 2 * LANES), lambda i, jj: (i, cidx(jj), 0)),
        ]

    fwd = lambda jj: jj
    bwd = lambda jj: steps - 1 - jj
    state_spec = pl.BlockSpec((bs, 2, n_stack, MXU_DIM, MXU_DIM), lambda i, jj: (i, 0, 0, 0, 0))
    in_specs = chunk_specs(fwd) + chunk_specs(bwd) + [
        pl.BlockSpec((8, c3), lambda i, jj: (0, 0)),
        pl.BlockSpec((2, 8, rdim), lambda i, jj: (0, 0, 0)),
        pl.BlockSpec((2, LANES, rdim), lambda i, jj: (0, 0, 0)),
        pl.BlockSpec((2, LANES, rdim), lambda i, jj: (0, 0, 0)),
        state_spec,
    ]
    out_specs, out_shapes = [], []
    if emit_y:
        out_specs += [
            pl.BlockSpec((bs, tb, rdim), lambda i, jj: (i, fwd(jj), 0)),
            pl.BlockSpec((bs, tb, rdim), lambda i, jj: (i, bwd(jj), 0)),
            pl.BlockSpec((bs, tb, rdim), lambda i, jj: (i, fwd(jj), 0)),
        ]
        out_shapes += [jax.ShapeDtypeStruct((b, l, rdim), F32)] * 3
    out_specs.append(state_spec)
    out_shapes.append(jax.ShapeDtypeStruct((b, 2, n_stack, MXU_DIM, MXU_DIM), F32))
    return pl.pallas_call(
        functools.partial(_wkv_kernel, n_chunks=nc, emit_y=emit_y),
        grid=(b // bs, steps),
        in_specs=in_specs,
        out_specs=out_specs,
        out_shape=out_shapes,
        scratch_shapes=[pltpu.VMEM((bs, 2, n_stack, MXU_DIM, MXU_DIM), F32)],
        compiler_params=_params(("parallel", "arbitrary")),
        name="wkv",
    )(*([z_rkv, z_rkv, z_rkv, z_lora] * 2), conv_w8, pvec, w2_pad, a2_pad, s0)


def _rope(x, cos, sin):
    lane = lax.broadcasted_iota(jnp.int32, x.shape, 1)
    first = (lane & 31) < 16
    partner = jnp.where(first, pltpu.roll(x, LANES - 16, 1), pltpu.roll(x, 16, 1))
    return x * cos + partner * sin


def _head_rmsnorm(x, g, ones_bd):
    ss = _head_sum(x * x, ones_bd)
    return x * lax.rsqrt(ss * (1.0 / HEAD_DIM) + NORM_EPS) * g


def _attn_kernel(q_ref, kv_ref, kvc_ref, cos_ref, sin_ref, qg_ref, kg_ref, sink_ref, o_ref,
                 kp_ref, vp_ref, kc_ref, vc_ref, *, seq_len, n_q_heads):
    blk = ATT_BLOCK
    i = pl.program_id(1)
    ones_bd = _head_ones(LANES)
    kg = kg_ref[...]
    n_kv = kv_ref.shape[-1] // (2 * HEAD_DIM)
    kw = n_kv * HEAD_DIM
    group = n_q_heads // n_kv
    lc = kvc_ref.shape[1]

    @pl.when(i == 0)
    def _():
        zeros = jnp.zeros((blk, kw), BF16)
        kp_ref[0:blk, :] = zeros
        vp_ref[0:blk, :] = zeros
        kp_ref[blk + seq_len:2 * blk + seq_len, :] = zeros
        vp_ref[blk + seq_len:2 * blk + seq_len, :] = zeros
        kk = _head_rmsnorm(kv_ref[0, :, 0:kw], kg, ones_bd)
        kp_ref[blk:blk + seq_len, :] = _rope(kk, cos_ref[...], sin_ref[...]).astype(BF16)
        vp_ref[blk:blk + seq_len, :] = kv_ref[0, :, kw:2 * kw].astype(BF16)
        kc_ref[...] = _head_rmsnorm(kvc_ref[0, :, 0:kw], kg, ones_bd).astype(BF16)
        vc_ref[...] = kvc_ref[0, :, kw:2 * kw].astype(BF16)

    k_ctx = kc_ref[...]
    v_ctx = vc_ref[...]
    lane = lax.broadcasted_iota(jnp.int32, (blk, LANES), 1)
    low_half = lane < HEAD_DIM
    high_half = lane >= HEAD_DIM
    scale = HEAD_DIM ** -0.5
    qr = lax.broadcasted_iota(jnp.int32, (blk, 3 * blk), 0)
    kc_i = lax.broadcasted_iota(jnp.int32, (blk, 3 * blk), 1)
    n_sub = q_ref.shape[1] // blk

    heads = []
    for sb in range(n_sub):
        start = pl.multiple_of((i * n_sub + sb) * blk, blk)
        rows = slice(sb * blk, (sb + 1) * blk)
        cos = cos_ref[pl.ds(start, blk), :]
        sin = sin_ref[pl.ds(start, blk), :]
        k_win = kp_ref[pl.ds(start, 3 * blk), :]
        v_win = vp_ref[pl.ds(start, 3 * blk), :]

        q_heads = []
        for m in range(n_q_heads // 2):
            slab = q_ref[0, rows, m * LANES:(m + 1) * LANES]
            slab = _rope(_head_rmsnorm(slab, qg_ref[...], ones_bd), cos, sin) * scale
            q_heads.append(slab)

        lo = jnp.maximum(qr, blk - start)
        hi = jnp.minimum(qr + 2 * WINDOW, seq_len + blk - 1 - start)
        hidden = jnp.where(kc_i < lo, 1.0, jnp.where(kc_i > hi, 1.0, 0.0))
        hidden = jnp.concatenate([hidden] * group, axis=0)

        for h in range(n_kv):
            kv_low = (h % 2) == 0
            parts = []
            for g in range(group):
                a = h * group + g
                slab = q_heads[a // 2]
                if ((a % 2) == 0) != kv_low:
                    slab = pltpu.roll(slab, HEAD_DIM, 1)
                parts.append(jnp.where(low_half if kv_low else high_half, slab, 0.0))
            sl = slice((h // 2) * LANES, (h // 2 + 1) * LANES)
            heads.append(dict(
                h=h, rows=rows, kv_low=kv_low, hidden=hidden,
                qg=jnp.concatenate(parts, axis=0).astype(BF16),
                k_all=jnp.concatenate([k_win[:, sl], k_ctx[:, sl]], axis=0),
                v_all=jnp.concatenate([v_win[:, sl], v_ctx[:, sl]], axis=0),
                sink=jnp.concatenate([jnp.full((blk, 1), sink_ref[h * group + g], F32) for g in range(group)],
                                     axis=0)))
    for hd in heads:
        s_all = _dot_nt(hd["qg"], hd["k_all"])
        hd["s"] = jnp.concatenate(
            [jnp.where(hd["hidden"] > 0.5, MASK_VALUE, s_all[:, 0:3 * blk]), s_all[:, 3 * blk:]], axis=1)
    for hd in heads:
        hd["mx"] = jnp.maximum(jnp.max(hd["s"], axis=-1, keepdims=True), hd["sink"])
    for hd in heads:
        hd["p"] = jnp.exp(hd["s"] - hd["mx"])
    for hd in heads:
        hd["den"] = jnp.sum(hd["p"], axis=-1, keepdims=True) + jnp.exp(hd["sink"] - hd["mx"])
    for hd in heads:
        hd["o"] = _dot(hd["p"].astype(BF16), hd["v_all"]) * (1.0 / hd["den"])
    for sb in range(n_sub):
        out_heads = [None] * n_q_heads
        for hd in heads[sb * n_kv:(sb + 1) * n_kv]:
            for g in range(group):
                a = hd["h"] * group + g
                o_a = hd["o"][g * blk:(g + 1) * blk]
                if ((a % 2) == 0) != hd["kv_low"]:
                    o_a = pltpu.roll(o_a, HEAD_DIM, 1)
                out_heads[a] = o_a
        rows = slice(sb * blk, (sb + 1) * blk)
        for m in range(n_q_heads // 2):
            o_ref[0, rows, m * LANES:(m + 1) * LANES] = jnp.where(low_half, out_heads[2 * m], out_heads[2 * m + 1])


def _attn(q, kv, kv_ctx, cos, sin, q_g, k_g, sink):
    b, l, qd = q.shape
    lc = kv_ctx.shape[1]
    kvd = kv.shape[-1]
    kw = kvd // 2
    n_q_heads = qd // HEAD_DIM
    blk = ATT_BLOCK
    tq = ATT_Q_BLOCKS_PER_STEP * blk if l % (ATT_Q_BLOCKS_PER_STEP * blk) == 0 else blk
    return pl.pallas_call(
        functools.partial(_attn_kernel, seq_len=l, n_q_heads=n_q_heads),
        grid=(b, l // tq),
        in_specs=[
            pl.BlockSpec((1, tq, qd), lambda i, t: (i, t, 0)),
            pl.BlockSpec((1, l, kvd), lambda i, t: (i, 0, 0)),
            pl.BlockSpec((1, lc, kvd), lambda i, t: (i, 0, 0)),
            pl.BlockSpec((l, LANES), lambda i, t: (0, 0)),
            pl.BlockSpec((l, LANES), lambda i, t: (0, 0)),
            pl.BlockSpec((1, LANES), lambda i, t: (0, 0)),
            pl.BlockSpec((1, LANES), lambda i, t: (0, 0)),
            pl.BlockSpec(memory_space=pltpu.SMEM),
        ],
        out_specs=pl.BlockSpec((1, tq, qd), lambda i, t: (i, t, 0)),
        out_shape=jax.ShapeDtypeStruct((b, l, qd), F32),
        scratch_shapes=[
            pltpu.VMEM((l + 2 * blk, kw), BF16),
            pltpu.VMEM((l + 2 * blk, kw), BF16),
            pltpu.VMEM((lc, kw), BF16),
            pltpu.VMEM((lc, kw), BF16),
        ],
        compiler_params=_params(("parallel", "arbitrary")),
        name="attn",
    )(q, kv, kv_ctx, cos, sin, q_g, k_g, sink)


def _rope_tables(seq_len):
    tpos = jnp.arange(seq_len, dtype=jnp.int32)
    row = (tpos // GRID_W).astype(F32)
    col = (tpos % GRID_W).astype(F32)
    dim = jnp.arange(HEAD_DIM, dtype=jnp.int32)
    half = HEAD_DIM // 4
    inv_freq = ROPE_BASE ** (-(dim % half).astype(F32) / half)
    pos = jnp.where((dim // (HEAD_DIM // 2))[None, :] == 0, row[:, None], col[:, None])
    ang = pos * inv_freq[None, :]
    sign = jnp.where((dim % (HEAD_DIM // 2)) < half, -1.0, 1.0)[None, :]
    cos = jnp.cos(ang)
    sin = jnp.sin(ang) * sign
    reps = LANES // HEAD_DIM
    return jnp.tile(cos, (1, reps)), jnp.tile(sin, (1, reps))


def _merge_kernel(yf_ref, yb_ref, bo_ref, gd_ref, at_ref, gates_ref, x_ref, mod_ref, lnw_ref, lnb_ref, g2_ref,
                  wbr_ref, wba_ref, wo_ref, n2_ref, rw_ref, x1_ref, h2_ref, aff_ref):
    d_model = x_ref.shape[-1]
    ones_bd = _head_ones(MXU_DIM)
    rdim = yf_ref.shape[-1]
    ys = yf_ref[0] + yb_ref[0]
    bonus = bo_ref[0]
    cols = []
    for q in range(rdim // MXU_DIM):
        sl = slice(q * MXU_DIM, (q + 1) * MXU_DIM)
        yq = ys[:, sl]
        mean = _head_sum(yq, ones_bd) * (1.0 / HEAD_DIM)
        diff = yq - mean
        var = _head_sum(diff * diff, ones_bd) * (1.0 / HEAD_DIM)
        cols.append(diff * lax.rsqrt(var + LNX_EPS))
    yn = jnp.concatenate(cols, axis=1)
    y = yn * lnw_ref[...] + lnb_ref[...] + bonus
    g = _dot(_sigmoid(gd_ref[0]).astype(BF16), g2_ref[...])
    o_rwkv = (y * g).astype(BF16)
    br = _dot(o_rwkv, wbr_ref[...])
    ba = _dot(at_ref[0].astype(BF16), wba_ref[...])
    gates = gates_ref[0]
    merged = _sigmoid(gates[:, 0:d_model]) * br + _sigmoid(gates[:, d_model:2 * d_model]) * ba
    out = _dot(merged.astype(BF16), wo_ref[...])
    x1 = x_ref[0] + mod_ref[0, 2:3, :] * out
    x1_ref[0] = x1
    ms = jnp.mean(x1 * x1, axis=-1, keepdims=True)
    h2 = x1 * lax.rsqrt(ms + NORM_EPS) * n2_ref[...]
    h2 = h2 * (1.0 + mod_ref[0, 4:5, :]) + mod_ref[0, 3:4, :]
    h_hi, h_lo = _split_bf16(h2, 2)
    h2_ref[0] = h_hi
    w_hi, w_lo = _split_bf16(rw_ref[...], 2)
    logits = _dot_nt(w_hi, h_hi) + _dot_nt(w_lo, h_hi) + _dot_nt(w_hi, h_lo)
    mx = jnp.max(logits, axis=0, keepdims=True)
    ex = jnp.exp(logits - mx)
    aff_ref[0] = ex / jnp.sum(ex, axis=0, keepdims=True)


def _merge(y_f, y_b, bonus, z_lora, at, gates, x, mod, lnx_w, lnx_b, g2, w_br_rwkv, w_br_attn, w_out,
           norm2_g, router_wt, tm):
    b, l, d = x.shape
    rdim = y_f.shape[-1]
    ne = router_wt.shape[0]
    const = lambda i, t: (0, 0)
    return pl.pallas_call(
        _merge_kernel,
        grid=(b, l // tm),
        in_specs=[
            pl.BlockSpec((1, tm, rdim), lambda i, t: (i, t, 0)),
            pl.BlockSpec((1, tm, rdim), lambda i, t: (i, t, 0)),
            pl.BlockSpec((1, tm, rdim), lambda i, t: (i, t, 0)),
            pl.BlockSpec((1, tm, LANES), lambda i, t: (i, t, 2)),
            pl.BlockSpec((1, tm, at.shape[-1]), lambda i, t: (i, t, 0)),
            pl.BlockSpec((1, tm, 2 * d), lambda i, t: (i, t, 0)),
            pl.BlockSpec((1, tm, d), lambda i, t: (i, t, 0)),
            pl.BlockSpec((1, 6, d), lambda i, t: (i, 0, 0)),
            pl.BlockSpec((1, rdim), const),
            pl.BlockSpec((1, rdim), const),
            pl.BlockSpec(g2.shape, const),
            pl.BlockSpec(w_br_rwkv.shape, const),
            pl.BlockSpec(w_br_attn.shape, const),
            pl.BlockSpec(w_out.shape, const),
            pl.BlockSpec((1, d), const),
            pl.BlockSpec(router_wt.shape, const),
        ],
        out_specs=[
            pl.BlockSpec((1, tm, d), lambda i, t: (i, t, 0)),
            pl.BlockSpec((1, tm, d), lambda i, t: (i, t, 0)),
            pl.BlockSpec((1, ne, tm), lambda i, t: (i, 0, t)),
        ],
        out_shape=[
            jax.ShapeDtypeStruct((b, l, d), F32),
            jax.ShapeDtypeStruct((b, l, d), BF16),
            jax.ShapeDtypeStruct((b, ne, l), F32),
        ],
        compiler_params=_params(("parallel", "parallel")),
        name="merge",
    )(y_f, y_b, bonus, z_lora, at, gates, x, mod, lnx_w.reshape(1, rdim), lnx_b.reshape(1, rdim), g2,
      w_br_rwkv, w_br_attn, w_out, norm2_g.reshape(1, d), router_wt)


def _prefix_count(mask_f, tri_excl):
    rows, l = mask_f.shape
    running = jnp.zeros((rows, 1), F32)
    pieces = []
    for blk in range(l // LANES):
        m = mask_f[:, blk * LANES:(blk + 1) * LANES]
        pieces.append(_dot(m.astype(BF16), tri_excl) + running)
        running = running + jnp.sum(m, axis=1, keepdims=True)
    return jnp.concatenate(pieces, axis=1)


def _route_kernel(aff_ref, sel_ref, *, cap):
    aff = aff_ref[0]
    bits = lax.bitcast_convert_type(aff, jnp.int32)
    thr = jnp.zeros((aff.shape[0], 1), jnp.int32)
    for bit in range(30, -1, -1):
        cand = thr | (1 << bit)
        cnt = jnp.sum(jnp.where(bits >= cand, 1.0, 0.0), axis=1, keepdims=True)
        thr = jnp.where(cnt >= cap, cand, thr)
    r = lax.broadcasted_iota(jnp.int32, (LANES, LANES), 0)
    c = lax.broadcasted_iota(jnp.int32, (LANES, LANES), 1)
    tri_excl = jnp.where(r < c, 1.0, 0.0).astype(BF16)
    gt = jnp.where(bits > thr, 1.0, 0.0)
    eq = jnp.where(bits == thr, 1.0, 0.0)
    need = cap - jnp.sum(gt, axis=1, keepdims=True)
    eq_rank = _prefix_count(eq, tri_excl)
    chosen = jnp.maximum(gt, jnp.where(eq_rank < need, eq, 0.0))
    rank = _prefix_count(chosen, tri_excl)
    sel_ref[0] = jnp.where(chosen > 0.5, rank, -1.0).astype(jnp.int32)


def _route(aff_t, cap):
    b, ne, l = aff_t.shape
    return pl.pallas_call(
        functools.partial(_route_kernel, cap=cap),
        grid=(b,),
        in_specs=[pl.BlockSpec((1, ne, l), lambda i: (i, 0, 0))],
        out_specs=pl.BlockSpec((1, ne, l), lambda i: (i, 0, 0)),
        out_shape=jax.ShapeDtypeStruct((b, ne, l), jnp.int32),
        compiler_params=_params(("parallel",)),
        name="route",
    )(aff_t)


def _moe_kernel(sel_ref, aff_ref, h_ref, x1_ref, mod_ref, wg_ref, wu_ref, wd_ref, o_ref, *, cap, scatter_tile):
    e = pl.program_id(1)
    l = h_ref.shape[1]

    @pl.when(e == 0)
    def _():
        o_ref[...] = x1_ref[...]

    sel_row = sel_ref[0, pl.ds(e, 1), :]
    aff_row = aff_ref[0, pl.ds(e, 1), :]
    slot = lax.broadcasted_iota(jnp.int32, (cap, l), 0)
    chosen = slot == sel_row
    onehot = jnp.where(chosen, 1.0, 0.0).astype(BF16)
    xe = _dot(onehot, h_ref[0]).astype(BF16)
    hg = _dot(xe, wg_ref[0])
    hu = _dot(xe, wu_ref[0])
    act = (hg * _sigmoid(hg) * hu).astype(BF16)
    ye = _dot(act, wd_ref[0])
    val = jnp.sum(jnp.where(chosen, aff_row, 0.0), axis=1, keepdims=True)
    yw = (ye * val * mod_ref[0, 5:6, :]).astype(BF16)
    for lt in range(l // scatter_tile):
        sl = slice(lt * scatter_tile, (lt + 1) * scatter_tile)
        o_ref[0, sl, :] += _dot_tn(onehot[:, sl], yw)


def _moe(sel, aff_t, h2, x1, mod, wg, wu, wd, cap):
    b, l, d = h2.shape
    ne = sel.shape[1]
    ff = wg.shape[-1]
    scatter_tile = min(l, 512)
    per_sample = lambda i, e: (i, 0, 0)
    per_expert = lambda i, e: (e, 0, 0)
    return pl.pallas_call(
        functools.partial(_moe_kernel, cap=cap, scatter_tile=scatter_tile),
        grid=(b, ne),
        in_specs=[
            pl.BlockSpec((1, ne, l), per_sample),
            pl.BlockSpec((1, ne, l), per_sample),
            pl.BlockSpec((1, l, d), per_sample, pipeline_mode=pl.Buffered(1)),
            pl.BlockSpec((1, l, d), per_sample, pipeline_mode=pl.Buffered(1)),
            pl.BlockSpec((1, 6, d), per_sample),
            pl.BlockSpec((1, d, ff), per_expert),
            pl.BlockSpec((1, d, ff), per_expert),
            pl.BlockSpec((1, ff, d), per_expert),
        ],
        out_specs=pl.BlockSpec((1, l, d), per_sample),
        out_shape=jax.ShapeDtypeStruct((b, l, d), F32),
        compiler_params=_params(("parallel", "arbitrary")),
        name="moe",
    )(sel, aff_t, h2, x1, mod, wg, wu, wd)


def _pad_rows(w, row0, rows):
    return jnp.zeros((rows, w.shape[1]), w.dtype).at[row0:row0 + w.shape[0]].set(w)


def _layer(x, ctx, mod_lat, mod_ctx, p):
    b, l, d = x.shape
    lc = ctx.shape[1]
    rdim = p["rwkv_w0"].shape[-1]
    dl = p["rwkv_w2"].shape[-2]
    al = p["rwkv_a2"].shape[-2]
    gl = p["rwkv_g2"].shape[0]
    att_dim = p["w_br_attn"].shape[0]
    kv_dim = (p["w_in"].shape[1] - 3 * rdim - 2 * dl - 2 * al - gl - att_dim - 2 * d) // 2
    assert 2 * dl == LANES and 2 * al == LANES and gl == LANES

    w_in = p["w_in"].astype(BF16)
    o = 0
    w_rkv = w_in[:, o:o + 3 * rdim]; o += 3 * rdim
    w_lora = w_in[:, o:o + 2 * dl + 2 * al + gl]; o += 2 * dl + 2 * al + gl
    w_q = w_in[:, o:o + att_dim]; o += att_dim
    w_kv = w_in[:, o:o + 2 * kv_dim]; o += 2 * kv_dim
    w_gates = w_in[:, o:o + 2 * d]

    tm = min(l, 512)
    z_rkv, z_lora, q, kv, gates = _proj(x, mod_lat, p["norm1_g"], [w_rkv, w_lora, w_q, w_kv, w_gates],
                                        min(l, 512), True)
    tmc = min(lc, 256)
    zc_rkv, zc_lora, kv_c = _proj(ctx, mod_ctx, p["norm1_g"], [w_rkv, w_lora, w_kv], tmc, False)

    conv_w8 = _pad_rows(p["rwkv_conv"], 0, 8)
    zeros3 = jnp.zeros((3, rdim), F32)
    pvec = jnp.stack([
        jnp.concatenate([p["rwkv_w0"][dd][None], p["rwkv_a0"][dd][None], p["rwkv_kk"][dd][None],
                         p["rwkv_ka"][dd][None], p["rwkv_rk"][dd].reshape(1, rdim), zeros3], axis=0)
        for dd in range(2)])
    w2_pad = jnp.stack([_pad_rows(p["rwkv_w2"][dd], dd * dl, LANES) for dd in range(2)]).astype(BF16)
    a2_pad = jnp.stack([_pad_rows(p["rwkv_a2"][dd], dd * al, LANES) for dd in range(2)]).astype(BF16)
    n_stack = rdim // MXU_DIM
    s0 = jnp.zeros((b, 2, n_stack, MXU_DIM, MXU_DIM), F32)
    (s_ctx,) = _wkv(zc_rkv, zc_lora, conv_w8, pvec, w2_pad, a2_pad, s0, False)
    y_f, y_b, bonus, _ = _wkv(z_rkv, z_lora, conv_w8, pvec, w2_pad, a2_pad, s_ctx, True)

    cos, sin = _rope_tables(l)
    reps = LANES // HEAD_DIM
    q_g = jnp.tile(p["q_norm_g"], reps).reshape(1, LANES)
    k_g = jnp.tile(p["k_norm_g"], reps).reshape(1, LANES)
    at = _attn(q, kv, kv_c, cos, sin, q_g, k_g, p["attn_sink"])

    x1, h2, aff_t = _merge(y_f, y_b, bonus, z_lora, at, gates, x, mod_lat, p["lnx_w"], p["lnx_b"],
                           p["rwkv_g2"].astype(BF16), p["w_br_rwkv"].astype(BF16),
                           p["w_br_attn"].astype(BF16), p["w_out"].astype(BF16), p["norm2_g"],
                           p["router_w"].T, tm)
    ne = p["router_w"].shape[1]
    cap = EC_FACTOR * l // ne
    sel = _route(aff_t, cap)
    return _moe(sel, aff_t, h2, x1, mod_lat, p["exp_w_gate"].astype(BF16), p["exp_w_up"].astype(BF16),
                p["exp_w_down"].astype(BF16), cap)


def kernel(x, c, ctx, c_ctx, ada_w, ada_b, norm1_g, norm2_g, w_in, rwkv_conv, rwkv_w0, rwkv_w2, rwkv_a0, rwkv_a2, rwkv_kk, rwkv_ka, rwkv_rk, rwkv_g2, lnx_w, lnx_b, q_norm_g, k_norm_g, attn_sink, w_br_rwkv, w_br_attn, w_out, router_w, exp_w_gate, exp_w_up, exp_w_down):
    b, l, d = x.shape
    depth = ada_w.shape[0]
    assert depth == 1, "the context stream is only read, never advanced, for a single layer"
    rows = -(-(b + 1) // 8) * 8
    c_all = jnp.zeros((rows, d), F32).at[:b].set(c).at[b].set(c_ctx)
    names = ["ada_w", "ada_b", "norm1_g", "norm2_g", "w_in", "rwkv_conv", "rwkv_w0", "rwkv_w2", "rwkv_a0",
             "rwkv_a2", "rwkv_kk", "rwkv_ka", "rwkv_rk", "rwkv_g2", "lnx_w", "lnx_b", "q_norm_g", "k_norm_g",
             "attn_sink", "w_br_rwkv", "w_br_attn", "w_out", "router_w", "exp_w_gate", "exp_w_up", "exp_w_down"]
    vals = [ada_w, ada_b, norm1_g, norm2_g, w_in, rwkv_conv, rwkv_w0, rwkv_w2, rwkv_a0, rwkv_a2, rwkv_kk,
            rwkv_ka, rwkv_rk, rwkv_g2, lnx_w, lnx_b, q_norm_g, k_norm_g, attn_sink, w_br_rwkv, w_br_attn,
            w_out, router_w, exp_w_gate, exp_w_up, exp_w_down]
    p = {n: v[0] for n, v in zip(names, vals)}
    mod = _adaln(c_all, p["ada_w"], p["ada_b"])
    mod_lat = mod[:b].reshape(b, 6, d)
    mod_ctx = mod[b:b + 1].reshape(1, 6, d)
    return _layer(x, ctx, mod_lat, mod_ctx, p)
```
